```python
import jax, jax.numpy as jnp
from jax import lax
import numpy as np

D_MODEL = 1024
BATCH = 8
SEQ = 8192
DEPTH = 2

HEAD_DIM = 64
W_A = D_MODEL // 4
W_B = D_MODEL // 2
W_C = D_MODEL // 4
NH_A = W_A // HEAD_DIM
NH_B = W_B // HEAD_DIM
NG_C = 4
G_C = W_C // NG_C
N_HEADS_TOTAL = NH_A + NH_B + W_C // HEAD_DIM
MIX_WIDTH = W_A + W_B + W_C
IN_COLS = 2 * W_A + 3 * W_B + W_C
CHUNK = 128
SB_BLOCK = 128
POOL_WINDOWS = (2, 4, 8, 16)
D_FF = ((8 * D_MODEL // 3) + 127) // 128 * 128
CONV_WIDTH = 3
EPS = 1e-6

kernel_name = "hybrid_sgu_stickbreak_pool_convffn"


def rmsnorm(x, g):
    xf = x.astype(jnp.float32)
    y = xf * lax.rsqrt(jnp.mean(xf * xf, axis=-1, keepdims=True) + EPS)
    return (y * g.astype(jnp.float32)).astype(x.dtype)


def chunked_sgu(a, sgu_norm_g, sgu_w, sgu_b):
    B, S, _ = a.shape
    u, v = jnp.split(jax.nn.gelu(a, approximate=False), 2, axis=-1)
    v = rmsnorm(v.reshape(B, S, NH_A, HEAD_DIM), sgu_norm_g.reshape(NH_A, HEAD_DIM))
    v = v.reshape(B, S // CHUNK, CHUNK, NH_A, HEAD_DIM)
    tril = jnp.tril(jnp.ones((CHUNK, CHUNK), dtype=bool))
    wm = jnp.where(tril[None], sgu_w, jnp.zeros_like(sgu_w)).astype(v.dtype)
    s = jnp.einsum('hts,bcshd->bcthd', wm, v)
    s = s + jnp.transpose(sgu_b)[None, None, :, :, None].astype(v.dtype)
    return u * s.reshape(B, S, W_A)


def _stick_breaking_block(qb, k, v, q_start):
    T = qb.shape[2]
    S = k.shape[2]
    z = jnp.einsum('bhtd,bhsd->bhts', qb, k).astype(jnp.float32) * (HEAD_DIM ** -0.5)
    t_idx = q_start + jnp.arange(T, dtype=jnp.int32)
    s_idx = jnp.arange(S, dtype=jnp.int32)
    mask = (s_idx[None, :] < t_idx[:, None])[None, None]
    log_1m = jnp.where(mask, jax.nn.log_sigmoid(-z), 0.0)
    tail = lax.cumsum(log_1m, axis=3, reverse=True) - log_1m
    log_a = jax.nn.log_sigmoid(z) + tail
    att = jnp.where(mask, jnp.exp(log_a), 0.0)
    out = jnp.einsum('bhts,bhsd->bhtd', att, v.astype(jnp.float32))
    return out.astype(qb.dtype)


def stick_breaking_attention(q, k, v):
    B, S, H, d = q.shape
    nb = S // SB_BLOCK
    qt = jnp.transpose(q, (0, 2, 1, 3)).reshape(B, H, nb, SB_BLOCK, d)
    qt = jnp.transpose(qt, (2, 0, 1, 3, 4))
    kt = jnp.transpose(k, (0, 2, 1, 3))
    vt = jnp.transpose(v, (0, 2, 1, 3))
    starts = jnp.arange(nb, dtype=jnp.int32) * SB_BLOCK
    out = lax.map(lambda args: _stick_breaking_block(args[0], kt, vt, args[1]), (qt, starts))
    out = jnp.transpose(out, (1, 0, 3, 2, 4))
    return out.reshape(B, S, H * d)


def causal_pool_mixer(p, pool_w, pool_scale):
    B, S, _ = p.shape
    pf = p.astype(jnp.float32)
    csum = jnp.cumsum(pf, axis=1)
    pos = jnp.arange(1, S + 1, dtype=jnp.float32)
    outs = []
    for g, w in enumerate(POOL_WINDOWS):
        c = csum[..., g * G_C:(g + 1) * G_C]
        shifted = jnp.pad(c, ((0, 0), (w, 0), (0, 0)))[:, :S]
        cnt = jnp.minimum(pos, float(w))[None, :, None]
        d = (c - shifted) / cnt - pf[..., g * G_C:(g + 1) * G_C]
        outs.append(jnp.einsum('bsc,cd->bsd', d, pool_w[g].astype(jnp.float32)))
    y = jnp.concatenate(outs, axis=-1) * pool_scale.astype(jnp.float32)
    return y.astype(p.dtype)


def conv_gated_ffn(h, w_up, conv_w, conv_b, w_down):
    S = h.shape[1]
    z = h @ w_up
    zp = jnp.pad(z, ((0, 0), (CONV_WIDTH - 1, 0), (0, 0)))
    zc = conv_b
    for k in range(CONV_WIDTH):
        zc = zc + zp[:, k:k + S] * conv_w[k]
    g, u = jnp.split(zc, 2, axis=-1)
    return (jax.nn.silu(g) * u) @ w_down


def setup_inputs(seed: int = 0) -> dict:
    key = jax.random.key(seed)
    ks = jax.random.split(key, 16)
    f32 = jnp.float32

    def nrm(k, shape, scale):
        return jax.random.normal(k, shape, f32) * scale

    return {
        "x": nrm(ks[0], (BATCH, SEQ, D_MODEL), 1.0),
        "norm1_g": 1.0 + nrm(ks[1], (DEPTH, D_MODEL), 0.05),
        "w_in": nrm(ks[2], (DEPTH, D_MODEL, IN_COLS), D_MODEL ** -0.5),
        "sgu_norm_g": 1.0 + nrm(ks[3], (DEPTH, W_A), 0.05),
        "sgu_w": nrm(ks[4], (DEPTH, NH_A, CHUNK, CHUNK), 0.05),
        "sgu_b": 1.0 + nrm(ks[5], (DEPTH, NH_A, CHUNK), 0.05),
        "pool_w": nrm(ks[6], (DEPTH, NG_C, G_C, G_C), G_C ** -0.5),
        "pool_scale": 1.0 + nrm(ks[7], (DEPTH, W_C), 0.05),
        "mix_norm_g": 1.0 + nrm(ks[8], (DEPTH, MIX_WIDTH), 0.05),
        "w_o": nrm(ks[9], (DEPTH, MIX_WIDTH, D_MODEL), MIX_WIDTH ** -0.5),
        "norm2_g": 1.0 + nrm(ks[10], (DEPTH, D_MODEL), 0.05),
        "w_up": nrm(ks[11], (DEPTH, D_MODEL, 2 * D_FF), D_MODEL ** -0.5),
        "conv_w": nrm(ks[12], (DEPTH, CONV_WIDTH, 2 * D_FF), CONV_WIDTH ** -0.5),
        "conv_b": nrm(ks[13], (DEPTH, 2 * D_FF), 0.01),
        "w_down": nrm(ks[14], (DEPTH, D_FF, D_MODEL), D_FF ** -0.5),
        "final_g": 1.0 + nrm(ks[15], (D_MODEL,), 0.05),
    }


def reference(x, norm1_g, w_in, sgu_norm_g, sgu_w, sgu_b, pool_w, pool_scale,
              mix_norm_g, w_o, norm2_g, w_up, conv_w, conv_b, w_down, final_g):
    B, S, D = x.shape
    for l in range(DEPTH):
        h = rmsnorm(x, norm1_g[l])
        proj = h @ w_in[l]
        a_in = proj[..., :2 * W_A]
        qkv = proj[..., 2 * W_A:2 * W_A + 3 * W_B]
        p_in = proj[..., 2 * W_A + 3 * W_B:]
        q, k, v = jnp.split(qkv.reshape(B, S, 3, NH_B, HEAD_DIM), 3, axis=2)
        y_a = chunked_sgu(a_in, sgu_norm_g[l], sgu_w[l], sgu_b[l])
        y_b = stick_breaking_attention(q[:, :, 0], k[:, :, 0], v[:, :, 0])
        y_c = causal_pool_mixer(p_in, pool_w[l], pool_scale[l])
        y = jnp.concatenate([y_a, y_b, y_c], axis=-1).reshape(B, S, N_HEADS_TOTAL, HEAD_DIM)
        y = rmsnorm(y, mix_norm_g[l].reshape(N_HEADS_TOTAL, HEAD_DIM)).reshape(B, S, MIX_WIDTH)
        x = x + y @ w_o[l]
        h = rmsnorm(x, norm2_g[l])
        x = x + conv_gated_ffn(h, w_up[l], conv_w[l], conv_b[l], w_down[l])
    return rmsnorm(x, final_g)
```

```python
import functools

import jax
import jax.numpy as jnp
from jax import lax
from jax.experimental import pallas as pl
from jax.experimental.pallas import tpu as pltpu

F32 = jnp.float32
BF16 = jnp.bfloat16

EPS = 1e-6
HEAD_DIM = 64
CHUNK = 128
POOL_WINDOWS = (2, 4, 8, 16)
HALO = 16
LANES = 128
MXU_N = 256
ROW_TILE = 512
VMEM_LIMIT = 56 * 1024 * 1024
LOG_WEIGHT_FLOOR = -104.0


def _split_bf16(x):
    hi = x.astype(BF16)
    lo = (x - hi.astype(F32)).astype(BF16)
    return hi, lo


def _row_rmsnorm(x, g):
    return x * lax.rsqrt(jnp.mean(x * x, axis=-1, keepdims=True) + EPS) * g


def _head_rmsnorm(y, g, gmat):
    y2 = y * y
    hi, lo = _split_bf16(y2)
    parts = []
    for c in range(y.shape[1] // MXU_N):
        sl = slice(c * MXU_N, (c + 1) * MXU_N)
        parts.append(jnp.dot(hi[:, sl], gmat, preferred_element_type=F32)
                     + jnp.dot(lo[:, sl], gmat, preferred_element_type=F32))
    ms = parts[0] if len(parts) == 1 else jnp.concatenate(parts, axis=1)
    return y * lax.rsqrt(ms + EPS) * g


def _const_spec(shape):
    return pl.BlockSpec(shape, lambda *_: (0,) * len(shape))


def _in_proj_kernel(x_ref, g_ref, w_ref, a_ref, q_ref, k_ref, v_ref, p_ref, *, w_a2, w_b):
    h = _row_rmsnorm(x_ref[...], g_ref[...]).astype(BF16)

    def proj(c0, c1):
        return jnp.dot(h, w_ref[:, c0:c1], preferred_element_type=F32)

    a_ref[...] = proj(0, w_a2)
    q_ref[...] = (proj(w_a2, w_a2 + w_b) * (HEAD_DIM ** -0.5)).astype(BF16)
    k_ref[...] = proj(w_a2 + w_b, w_a2 + 2 * w_b).astype(BF16)
    v_ref[...] = proj(w_a2 + 2 * w_b, w_a2 + 3 * w_b).astype(BF16)
    p_ref[...] = proj(w_a2 + 3 * w_b, w_ref.shape[1])


def _in_proj(x, g, w, w_a2, w_b, w_c):
    n, d = x.shape
    tm = ROW_TILE
    row = lambda c: pl.BlockSpec((tm, c), lambda i: (i, 0))
    return pl.pallas_call(
        functools.partial(_in_proj_kernel, w_a2=w_a2, w_b=w_b),
        grid=(n // tm,),
        in_specs=[row(d), _const_spec((1, d)), _const_spec(w.shape)],
        out_specs=[row(w_a2), row(w_b), row(w_b), row(w_b), row(w_c)],
        out_shape=[jax.ShapeDtypeStruct((n, w_a2), F32),
                   jax.ShapeDtypeStruct((n, w_b), BF16),
                   jax.ShapeDtypeStruct((n, w_b), BF16),
                   jax.ShapeDtypeStruct((n, w_b), BF16),
                   jax.ShapeDtypeStruct((n, w_c), F32)],
        compiler_params=pltpu.CompilerParams(dimension_semantics=("arbitrary",), vmem_limit_bytes=VMEM_LIMIT),
        name="in_proj",
    )(x, g, w)


def _attn_kernel(q_ref, k_ref, v_ref, m_ref, o_ref, acc_ref, carry_ref, *, n_heads):
    qi = pl.program_id(1)
    lane = lax.broadcasted_iota(jnp.int32, (CHUNK, LANES), 1)
    row = lax.broadcasted_iota(jnp.int32, (CHUNK, LANES), 0)
    first_head = lane < HEAD_DIM
    causal = lane < row
    acc_ref[...] = jnp.zeros_like(acc_ref)
    carry_ref[...] = jnp.zeros_like(carry_ref)
    mcat = m_ref[...]
    nt_dims = (((1,), (1,)), ((), ()))

    def block(kb, diagonal):
        k0 = pl.multiple_of(kb * CHUNK, CHUNK)
        cmax = None
        for j in range(n_heads // 2):
            cols = slice(j * LANES, (j + 1) * LANES)
            q2 = q_ref[0, :, cols]
            k2 = k_ref[0, pl.ds(k0, CHUNK), cols]
            v2 = v_ref[0, pl.ds(k0, CHUNK), cols]
            outs = []
            for half in range(2):
                h = 2 * j + half
                qm = jnp.where(first_head if half == 0 else jnp.logical_not(first_head), q2, jnp.zeros_like(q2))
                z = lax.dot_general(qm, k2, nt_dims, preferred_element_type=F32)
                log_1m = jnp.minimum(-z, 0.0) - jnp.log(1.0 + jnp.exp(-jnp.abs(z)))
                if diagonal:
                    log_1m = jnp.where(causal, log_1m, 0.0)
                hi, lo = _split_bf16(log_1m)
                res = jnp.dot(jnp.concatenate([hi, lo], axis=1), mcat, preferred_element_type=F32)
                c = carry_ref[h]
                att = jnp.exp(z + res[:, :LANES] + c)
                if diagonal:
                    att = jnp.where(causal, att, 0.0)
                c = c + res[:, LANES:]
                carry_ref[h] = c
                cmax = c if cmax is None else jnp.maximum(cmax, c)
                outs.append(jnp.dot(att.astype(BF16), v2, preferred_element_type=F32))
            acc_ref[:, cols] += jnp.where(first_head, outs[0], outs[1])
        return jnp.max(cmax)

    cm = block(qi, True)

    def cond(state):
        kb, cm = state
        return jnp.logical_and(kb >= 0, cm > LOG_WEIGHT_FLOOR)

    def body(state):
        kb, _ = state
        return kb - 1, block(kb, False)

    lax.while_loop(cond, body, (qi - 1, cm))
    o_ref[0] = acc_ref[...]


def _attention(q, k, v, mcat):
    b, s, w = q.shape
    n_heads = w // HEAD_DIM
    return pl.pallas_call(
        functools.partial(_attn_kernel, n_heads=n_heads),
        grid=(b, s // CHUNK),
        in_specs=[pl.BlockSpec((1, CHUNK, w), lambda bi, qi: (bi, qi, 0)),
                  pl.BlockSpec((1, s, w), lambda bi, qi: (bi, 0, 0)),
                  pl.BlockSpec((1, s, w), lambda bi, qi: (bi, 0, 0)),
                  _const_spec(mcat.shape)],
        out_specs=pl.BlockSpec((1, CHUNK, w), lambda bi, qi: (bi, qi, 0)),
        out_shape=jax.ShapeDtypeStruct((b, s, w), F32),
        scratch_shapes=[pltpu.VMEM((CHUNK, w), F32), pltpu.VMEM((n_heads, CHUNK, LANES), F32)],
        compiler_params=pltpu.CompilerParams(dimension_semantics=("arbitrary", "arbitrary"),
                                             vmem_limit_bytes=VMEM_LIMIT),
        name="attention",
    )(q, k, v, mcat)


def _mixer_out_kernel(x_ref, a_ref, yb_ref, p_ref, pprev_ref, sg_ref, sw_ref, sb_ref, pw_ref, ps_ref,
                      mg_ref, wo_ref, gm_ref, o_ref, *, tiles_per_seq):
    i = pl.program_id(0)
    tm = x_ref.shape[0]
    w_a = a_ref.shape[1] // 2
    w_c = p_ref.shape[1]
    gmat = gm_ref[...]
    seq_tile = i % tiles_per_seq

    a = a_ref[...]
    ga = 0.5 * a * (1.0 + lax.erf(a * (2.0 ** -0.5)))
    u = ga[:, :w_a]
    vn = _head_rmsnorm(ga[:, w_a:], sg_ref[...], gmat).astype(BF16)
    n_h = w_a // HEAD_DIM
    r = lax.broadcasted_iota(jnp.int32, (CHUNK, CHUNK), 0)
    cidx = lax.broadcasted_iota(jnp.int32, (CHUNK, CHUNK), 1)
    wms = [jnp.where(cidx <= r, sw_ref[h], 0.0).astype(BF16) for h in range(n_h)]
    head_of_lane = lax.broadcasted_iota(jnp.int32, (CHUNK, w_a), 1) // HEAD_DIM
    ya_parts = []
    for c in range(tm // CHUNK):
        vb = vn[c * CHUNK:(c + 1) * CHUNK]
        s = sb_ref[...]
        for h in range(n_h):
            sh = jnp.dot(wms[h], vb, preferred_element_type=F32)
            s = s + jnp.where(head_of_lane == h, sh, 0.0)
        ya_parts.append(u[c * CHUNK:(c + 1) * CHUNK] * s)
    ya = jnp.concatenate(ya_parts, axis=0)

    p = p_ref[...]
    pprev = jnp.where(seq_tile == 0, 0.0, pprev_ref[...])
    acc = jnp.concatenate([pprev, p], axis=0)
    group = lax.broadcasted_iota(jnp.int32, (tm, w_c), 1) // (w_c // len(POOL_WINDOWS))
    pos1 = (seq_tile * tm + lax.broadcasted_iota(jnp.int32, (tm, w_c), 0) + 1).astype(F32)
    win = jnp.zeros((tm, w_c), F32)
    cnt = jnp.ones((tm, w_c), F32)
    span = 1
    for g, wdw in enumerate(POOL_WINDOWS):
        while span < wdw:
            acc = acc + pltpu.roll(acc, span, 0)
            span *= 2
        win = jnp.where(group == g, acc[HALO:], win)
        cnt = jnp.where(group == g, jnp.minimum(pos1, float(wdw)), cnt)
    d = win / cnt - p
    yc = jnp.dot(d.astype(BF16), pw_ref[...], preferred_element_type=F32) * ps_ref[...]

    y = jnp.concatenate([ya, yb_ref[...], yc], axis=1)
    yn = _head_rmsnorm(y, mg_ref[...], gmat).astype(BF16)
    o_ref[...] = x_ref[...] + jnp.dot(yn, wo_ref[...], preferred_element_type=F32)


def _mixer_out(x, a, yb, p, sg, sw, sb_full, pw_bd, ps, mg, wo, gmat, seq_len):
    n, d = x.shape
    tm = ROW_TILE
    hb = tm // HALO
    row = lambda c: pl.BlockSpec((tm, c), lambda i: (i, 0))
    prev = pl.BlockSpec((HALO, p.shape[1]), lambda i: (jnp.maximum(i * hb - 1, 0), 0))
    return pl.pallas_call(
        functools.partial(_mixer_out_kernel, tiles_per_seq=seq_len // tm),
        grid=(n // tm,),
        in_specs=[row(d), row(a.shape[1]), row(yb.shape[1]), row(p.shape[1]), prev,
                  _const_spec(sg.shape), _const_spec(sw.shape), _const_spec(sb_full.shape),
                  _const_spec(pw_bd.shape), _const_spec(ps.shape), _const_spec(mg.shape),
                  _const_spec(wo.shape), _const_spec(gmat.shape)],
        out_specs=row(d),
        out_shape=jax.ShapeDtypeStruct((n, d), F32),
        compiler_params=pltpu.CompilerParams(dimension_semantics=("arbitrary",), vmem_limit_bytes=VMEM_LIMIT),
        name="mixer_out",
    )(x, a, yb, p, p, sg, sw, sb_full, pw_bd, ps, mg, wo, gmat)


def _ffn_kernel(x_ref, xprev_ref, g_ref, wup_ref, cw_ref, cb_ref, wdn_ref, fg_ref, o_ref, acc_ref,
                *, tiles_per_seq, final_norm):
    i = pl.program_id(0)
    tm = x_ref.shape[0]
    fc = wdn_ref.shape[1]
    x = x_ref[...]
    h = _row_rmsnorm(x, g_ref[...])
    hprev = _row_rmsnorm(xprev_ref[...], g_ref[...])
    hprev = jnp.where(i % tiles_per_seq == 0, 0.0, hprev)
    hext = jnp.concatenate([hprev, h], axis=0).astype(BF16)
    acc_ref[...] = jnp.zeros_like(acc_ref)

    def chunk(c, carry):
        z = jnp.dot(hext, wup_ref[c], preferred_element_type=F32)
        cw = cw_ref[c]
        zc = (cb_ref[c] + pltpu.roll(z, 2, 0)[HALO:] * cw[0:1] + pltpu.roll(z, 1, 0)[HALO:] * cw[1:2]
              + z[HALO:] * cw[2:3])
        gate = zc[:, :fc]
        act = (gate * jax.nn.sigmoid(gate) * zc[:, fc:]).astype(BF16)
        acc_ref[...] += jnp.dot(act, wdn_ref[c], preferred_element_type=F32)
        return carry

    lax.fori_loop(0, wup_ref.shape[0], chunk, 0)
    out = x + acc_ref[...]
    if final_norm:
        out = _row_rmsnorm(out, fg_ref[...])
    o_ref[...] = out


def _ffn(x, g, wup_c, cw_c, cb_c, wdn_c, fg, seq_len, final_norm):
    n, d = x.shape
    tm = ROW_TILE
    hb = tm // HALO
    row = pl.BlockSpec((tm, d), lambda i: (i, 0))
    prev = pl.BlockSpec((HALO, d), lambda i: (jnp.maximum(i * hb - 1, 0), 0))
    return pl.pallas_call(
        functools.partial(_ffn_kernel, tiles_per_seq=seq_len // tm, final_norm=final_norm),
        grid=(n // tm,),
        in_specs=[row, prev, _const_spec(g.shape), _const_spec(wup_c.shape), _const_spec(cw_c.shape),
                  _const_spec(cb_c.shape), _const_spec(wdn_c.shape), _const_spec(fg.shape)],
        out_specs=row,
        out_shape=jax.ShapeDtypeStruct((n, d), F32),
        scratch_shapes=[pltpu.VMEM((tm, d), F32)],
        compiler_params=pltpu.CompilerParams(dimension_semantics=("arbitrary",), vmem_limit_bytes=VMEM_LIMIT),
        name="ffn",
    )(x, x, g, wup_c, cw_c, cb_c, wdn_c, fg)


def _constants():
    j = jnp.arange(CHUNK)
    tail = (j[:, None] >= j[None, :]).astype(BF16)
    half = jnp.concatenate([tail, jnp.ones((CHUNK, LANES), BF16)], axis=1)
    mcat = jnp.concatenate([half, half], axis=0)
    i = jnp.arange(MXU_N)
    gmat = jnp.where(i[:, None] // HEAD_DIM == i[None, :] // HEAD_DIM, 1.0 / HEAD_DIM, 0.0).astype(BF16)
    return mcat, gmat


def kernel(x, norm1_g, w_in, sgu_norm_g, sgu_w, sgu_b, pool_w, pool_scale, mix_norm_g, w_o, norm2_g, w_up, conv_w,
           conv_b, w_down, final_g):
    b, s, d = x.shape
    depth = w_in.shape[0]
    w_a = sgu_norm_g.shape[1]
    w_c = pool_scale.shape[1]
    w_b = (w_in.shape[2] - 2 * w_a - w_c) // 3
    d_ff = w_down.shape[1]
    fc = MXU_N
    n_fc = d_ff // fc
    assert d_ff % fc == 0 and s % ROW_TILE == 0 and w_a + w_b + w_c == d
    mcat, gmat = _constants()

    xf = x.reshape(b * s, d)
    for l in range(depth):
        a, q, k, v, p = _in_proj(xf, norm1_g[l][None], w_in[l].astype(BF16), 2 * w_a, w_b, w_c)
        yb = _attention(q.reshape(b, s, w_b), k.reshape(b, s, w_b), v.reshape(b, s, w_b), mcat)
        sb_full = jnp.repeat(sgu_b[l].T, HEAD_DIM, axis=1)
        pw_bd = jax.scipy.linalg.block_diag(*[pool_w[l, g] for g in range(pool_w.shape[1])]).astype(BF16)
        x1 = _mixer_out(xf, a, yb.reshape(b * s, w_b), p, sgu_norm_g[l][None], sgu_w[l], sb_full, pw_bd,
                        pool_scale[l][None], mix_norm_g[l][None], w_o[l].astype(BF16), gmat, s)
        wup_c = w_up[l].reshape(d, 2, n_fc, fc).transpose(2, 0, 1, 3).reshape(n_fc, d, 2 * fc).astype(BF16)
        cw_c = conv_w[l].reshape(-1, 2, n_fc, fc).transpose(2, 0, 1, 3).reshape(n_fc, -1, 2 * fc)
        cb_c = conv_b[l].reshape(2, n_fc, fc).transpose(1, 0, 2).reshape(n_fc, 1, 2 * fc)
        wdn_c = w_down[l].reshape(n_fc, fc, d).astype(BF16)
        xf = _ffn(x1, norm2_g[l][None], wup_c, cw_c, cb_c, wdn_c, final_g[None], s, l == depth - 1)
    return xf.reshape(b, s, d)
```

```python
import functools

import jax
import jax.numpy as jnp
from jax import lax
from jax.experimental import pallas as pl
from jax.experimental.pallas import tpu as pltpu

F32 = jnp.float32
BF16 = jnp.bfloat16

EPS = 1e-6
HEAD_DIM = 64
CHUNK = 128
POOL_WINDOWS = (2, 4, 8, 16)
HALO = 16
LANES = 128
MXU_N = 256
ROW_TILE = 512
VMEM_LIMIT = 56 * 1024 * 1024
LOG_WEIGHT_FLOOR = -104.0


def _split_bf16(x):
    hi = x.astype(BF16)
    lo = (x - hi.astype(F32)).astype(BF16)
    return hi, lo


def _row_rmsnorm(x, g):
    return x * lax.rsqrt(jnp.mean(x * x, axis=-1, keepdims=True) + EPS) * g


def _head_rmsnorm(y, g, gmat):
    y2 = y * y
    hi, lo = _split_bf16(y2)
    parts = []
    for c in range(y.shape[1] // MXU_N):
        sl = slice(c * MXU_N, (c + 1) * MXU_N)
        parts.append(jnp.dot(hi[:, sl], gmat, preferred_element_type=F32)
                     + jnp.dot(lo[:, sl], gmat, preferred_element_type=F32))
    ms = parts[0] if len(parts) == 1 else jnp.concatenate(parts, axis=1)
    return y * lax.rsqrt(ms + EPS) * g


def _const_spec(shape):
    return pl.BlockSpec(shape, lambda *_: (0,) * len(shape))


def _in_proj_kernel(x_ref, g_ref, w_ref, a_ref, q_ref, k_ref, v_ref, p_ref, *, w_a2, w_b):
    h = _row_rmsnorm(x_ref[...], g_ref[...]).astype(BF16)

    def proj(c0, c1):
        return jnp.dot(h, w_ref[:, c0:c1], preferred_element_type=F32)

    a_ref[...] = proj(0, w_a2)
    q_ref[...] = (proj(w_a2, w_a2 + w_b) * (HEAD_DIM ** -0.5)).astype(BF16)
    k_ref[...] = proj(w_a2 + w_b, w_a2 + 2 * w_b).astype(BF16)
    v_ref[...] = proj(w_a2 + 2 * w_b, w_a2 + 3 * w_b).astype(BF16)
    p_ref[...] = proj(w_a2 + 3 * w_b, w_ref.shape[1])


def _in_proj(x, g, w, w_a2, w_b, w_c):
    n, d = x.shape
    tm = ROW_TILE
    row = lambda c: pl.BlockSpec((tm, c), lambda i: (i, 0))
    return pl.pallas_call(
        functools.partial(_in_proj_kernel, w_a2=w_a2, w_b=w_b),
        grid=(n // tm,),
        in_specs=[row(d), _const_spec((1, d)), _const_spec(w.shape)],
        out_specs=[row(w_a2), row(w_b), row(w_b), row(w_b), row(w_c)],
        out_shape=[jax.ShapeDtypeStruct((n, w_a2), F32),
                   jax.ShapeDtypeStruct((n, w_b), BF16),
                   jax.ShapeDtypeStruct((n, w_b), BF16),
                   jax.ShapeDtypeStruct((n, w_b), BF16),
                   jax.ShapeDtypeStruct((n, w_c), F32)],
        compiler_params=pltpu.CompilerParams(dimension_semantics=("arbitrary",), vmem_limit_bytes=VMEM_LIMIT),
        name="in_proj",
    )(x, g, w)


def _attn_kernel(q_ref, k_ref, v_ref, m_ref, o_ref, qs_ref, z_ref, l_ref, att_ref, carry_ref, acc_ref, *, n_heads):
    qi = pl.program_id(1)
    n_pairs = n_heads // 2
    pair_rows = 2 * CHUNK
    lane = lax.broadcasted_iota(jnp.int32, (pair_rows, LANES), 1)
    row = lax.broadcasted_iota(jnp.int32, (pair_rows, LANES), 0)
    own_lanes = (lane < HEAD_DIM) == (row < CHUNK)
    causal = lane < (row % CHUNK)
    first_head = lax.broadcasted_iota(jnp.int32, (CHUNK, LANES), 1) < HEAD_DIM
    nt_dims = (((1,), (1,)), ((), ()))

    for j in range(n_pairs):
        q2 = q_ref[0, :, j * LANES:(j + 1) * LANES]
        q4 = jnp.concatenate([q2, q2], axis=0)
        qs_ref[j * pair_rows:(j + 1) * pair_rows] = jnp.where(own_lanes, q4, jnp.zeros_like(q4))
    acc_ref[...] = jnp.zeros_like(acc_ref)
    carry_ref[...] = jnp.zeros_like(carry_ref)

    def scores(kb, diagonal):
        k0 = pl.multiple_of(kb * CHUNK, CHUNK)
        for j in range(n_pairs):
            rows = slice(j * pair_rows, (j + 1) * pair_rows)
            k2 = k_ref[0, pl.ds(k0, CHUNK), j * LANES:(j + 1) * LANES]
            z = lax.dot_general(qs_ref[rows], k2, nt_dims, preferred_element_type=F32)
            log_1m = jnp.minimum(-z, 0.0) - jnp.log(1.0 + jnp.exp(-jnp.abs(z)))
            if diagonal:
                log_1m = jnp.where(causal, log_1m, 0.0)
            hi, lo = _split_bf16(log_1m)
            z_ref[rows] = z
            l_ref[rows, :LANES] = hi
            l_ref[rows, LANES:] = lo

    def weights(diagonal):
        cmax = None
        for j in range(n_pairs):
            rows = slice(j * pair_rows, (j + 1) * pair_rows)
            res = jnp.dot(l_ref[rows], m_ref[...], preferred_element_type=F32)
            c = carry_ref[rows]
            att = jnp.exp(z_ref[rows] + res[:, :LANES] + c)
            if diagonal:
                att = jnp.where(causal, att, 0.0)
            att_ref[rows] = att.astype(BF16)
            c = c + res[:, LANES:]
            carry_ref[rows] = c
            cmax = c if cmax is None else jnp.maximum(cmax, c)
        return jnp.max(cmax)

    def values(kb):
        k0 = pl.multiple_of(kb * CHUNK, CHUNK)
        for j in range(n_pairs):
            rows = slice(j * pair_rows, (j + 1) * pair_rows)
            cols = slice(j * LANES, (j + 1) * LANES)
            v2 = v_ref[0, pl.ds(k0, CHUNK), cols]
            o2 = jnp.dot(att_ref[rows], v2, preferred_element_type=F32)
            acc_ref[:, cols] += jnp.where(first_head, o2[:CHUNK], o2[CHUNK:])

    scores(qi, True)
    cm = weights(True)

    def cond(state):
        kb, cm = state
        return jnp.logical_and(kb >= 0, cm > LOG_WEIGHT_FLOOR)

    def body(state):
        kb, _ = state
        scores(kb, False)
        values(kb + 1)
        return kb - 1, weights(False)

    kb_last, _ = lax.while_loop(cond, body, (qi - 1, cm))
    values(kb_last + 1)
    o_ref[0] = acc_ref[...]


def _attention(q, k, v, mcat):
    b, s, w = q.shape
    n_heads = w // HEAD_DIM
    rows = n_heads * CHUNK
    return pl.pallas_call(
        functools.partial(_attn_kernel, n_heads=n_heads),
        grid=(b, s // CHUNK),
        in_specs=[pl.BlockSpec((1, CHUNK, w), lambda bi, qi: (bi, qi, 0)),
                  pl.BlockSpec((1, s, w), lambda bi, qi: (bi, 0, 0)),
                  pl.BlockSpec((1, s, w), lambda bi, qi: (bi, 0, 0)),
                  _const_spec(mcat.shape)],
        out_specs=pl.BlockSpec((1, CHUNK, w), lambda bi, qi: (bi, qi, 0)),
        out_shape=jax.ShapeDtypeStruct((b, s, w), F32),
        scratch_shapes=[pltpu.VMEM((rows, LANES), BF16),
                        pltpu.VMEM((rows, LANES), F32),
                        pltpu.VMEM((rows, 2 * LANES), BF16),
                        pltpu.VMEM((rows, LANES), BF16),
                        pltpu.VMEM((rows, LANES), F32),
                        pltpu.VMEM((CHUNK, w), F32)],
        compiler_params=pltpu.CompilerParams(dimension_semantics=("arbitrary", "arbitrary"),
                                             vmem_limit_bytes=VMEM_LIMIT),
        name="attention",
    )(q, k, v, mcat)


def _mixer_out_kernel(x_ref, a_ref, yb_ref, p_ref, pprev_ref, sg_ref, sw_ref, sb_ref, pw_ref, ps_ref,
                      mg_ref, wo_ref, gm_ref, o_ref, *, tiles_per_seq):
    i = pl.program_id(0)
    tm = x_ref.shape[0]
    w_a = a_ref.shape[1] // 2
    w_c = p_ref.shape[1]
    gmat = gm_ref[...]
    seq_tile = i % tiles_per_seq

    a = a_ref[...]
    ga = 0.5 * a * (1.0 + lax.erf(a * (2.0 ** -0.5)))
    u = ga[:, :w_a]
    vn = _head_rmsnorm(ga[:, w_a:], sg_ref[...], gmat).astype(BF16)
    n_h = w_a // HEAD_DIM
    r = lax.broadcasted_iota(jnp.int32, (CHUNK, CHUNK), 0)
    cidx = lax.broadcasted_iota(jnp.int32, (CHUNK, CHUNK), 1)
    wms = [jnp.where(cidx <= r, sw_ref[h], 0.0).astype(BF16) for h in range(n_h)]
    head_of_lane = lax.broadcasted_iota(jnp.int32, (CHUNK, w_a), 1) // HEAD_DIM
    ya_parts = []
    for c in range(tm // CHUNK):
        vb = vn[c * CHUNK:(c + 1) * CHUNK]
        s = sb_ref[...]
        for h in range(n_h):
            sh = jnp.dot(wms[h], vb, preferred_element_type=F32)
            s = s + jnp.where(head_of_lane == h, sh, 0.0)
        ya_parts.append(u[c * CHUNK:(c + 1) * CHUNK] * s)
    ya = jnp.concatenate(ya_parts, axis=0)

    p = p_ref[...]
    pprev = jnp.where(seq_tile == 0, 0.0, pprev_ref[...])
    acc = jnp.concatenate([pprev, p], axis=0)
    group = lax.broadcasted_iota(jnp.int32, (tm, w_c), 1) // (w_c // len(POOL_WINDOWS))
    pos1 = (seq_tile * tm + lax.broadcasted_iota(jnp.int32, (tm, w_c), 0) + 1).astype(F32)
    win = jnp.zeros((tm, w_c), F32)
    cnt = jnp.ones((tm, w_c), F32)
    span = 1
    for g, wdw in enumerate(POOL_WINDOWS):
        while span < wdw:
            acc = acc + pltpu.roll(acc, span, 0)
            span *= 2
        win = jnp.where(group == g, acc[HALO:], win)
        cnt = jnp.where(group == g, jnp.minimum(pos1, float(wdw)), cnt)
    d = win / cnt - p
    yc = jnp.dot(d.astype(BF16), pw_ref[...], preferred_element_type=F32) * ps_ref[...]

    y = jnp.concatenate([ya, yb_ref[...], yc], axis=1)
    yn = _head_rmsnorm(y, mg_ref[...], gmat).astype(BF16)
    o_ref[...] = x_ref[...] + jnp.dot(yn, wo_ref[...], preferred_element_type=F32)


def _mixer_out(x, a, yb, p, sg, sw, sb_full, pw_bd, ps, mg, wo, gmat, seq_len):
    n, d = x.shape
    tm = ROW_TILE
    hb = tm // HALO
    row = lambda c: pl.BlockSpec((tm, c), lambda i: (i, 0))
    prev = pl.BlockSpec((HALO, p.shape[1]), lambda i: (jnp.maximum(i * hb - 1, 0), 0))
    return pl.pallas_call(
        functools.partial(_mixer_out_kernel, tiles_per_seq=seq_len // tm),
        grid=(n // tm,),
        in_specs=[row(d), row(a.shape[1]), row(yb.shape[1]), row(p.shape[1]), prev,
                  _const_spec(sg.shape), _const_spec(sw.shape), _const_spec(sb_full.shape),
                  _const_spec(pw_bd.shape), _const_spec(ps.shape), _const_spec(mg.shape),
                  _const_spec(wo.shape), _const_spec(gmat.shape)],
        out_specs=row(d),
        out_shape=jax.ShapeDtypeStruct((n, d), F32),
        compiler_params=pltpu.CompilerParams(dimension_semantics=("arbitrary",), vmem_limit_bytes=VMEM_LIMIT),
        name="mixer_out",
    )(x, a, yb, p, p, sg, sw, sb_full, pw_bd, ps, mg, wo, gmat)


def _ffn_kernel(x_ref, xprev_ref, g_ref, wup_ref, cw_ref, cb_ref, wdn_ref, fg_ref, o_ref, acc_ref,
                *, tiles_per_seq, final_norm):
    i = pl.program_id(0)
    tm = x_ref.shape[0]
    fc = wdn_ref.shape[1]
    x = x_ref[...]
    h = _row_rmsnorm(x, g_ref[...])
    hprev = _row_rmsnorm(xprev_ref[...], g_ref[...])
    hprev = jnp.where(i % tiles_per_seq == 0, 0.0, hprev)
    hext = jnp.concatenate([hprev, h], axis=0).astype(BF16)
    acc_ref[...] = jnp.zeros_like(acc_ref)

    def chunk(c, carry):
        z = jnp.dot(hext, wup_ref[c], preferred_element_type=F32)
        cw = cw_ref[c]
        zc = (cb_ref[c] + pltpu.roll(z, 2, 0)[HALO:] * cw[0:1] + pltpu.roll(z, 1, 0)[HALO:] * cw[1:2]
              + z[HALO:] * cw[2:3])
        gate = zc[:, :fc]
        act = (gate * jax.nn.sigmoid(gate) * zc[:, fc:]).astype(BF16)
        acc_ref[...] += jnp.dot(act, wdn_ref[c], preferred_element_type=F32)
        return carry

    lax.fori_loop(0, wup_ref.shape[0], chunk, 0)
    out = x + acc_ref[...]
    if final_norm:
        out = _row_rmsnorm(out, fg_ref[...])
    o_ref[...] = out


def _ffn(x, g, wup_c, cw_c, cb_c, wdn_c, fg, seq_len, final_norm):
    n, d = x.shape
    tm = ROW_TILE
    hb = tm // HALO
    row = pl.BlockSpec((tm, d), lambda i: (i, 0))
    prev = pl.BlockSpec((HALO, d), lambda i: (jnp.maximum(i * hb - 1, 0), 0))
    return pl.pallas_call(
        functools.partial(_ffn_kernel, tiles_per_seq=seq_len // tm, final_norm=final_norm),
        grid=(n // tm,),
        in_specs=[row, prev, _const_spec(g.shape), _const_spec(wup_c.shape), _const_spec(cw_c.shape),
                  _const_spec(cb_c.shape), _const_spec(wdn_c.shape), _const_spec(fg.shape)],
        out_specs=row,
        out_shape=jax.ShapeDtypeStruct((n, d), F32),
        scratch_shapes=[pltpu.VMEM((tm, d), F32)],
        compiler_params=pltpu.CompilerParams(dimension_semantics=("arbitrary",), vmem_limit_bytes=VMEM_LIMIT),
        name="ffn",
    )(x, x, g, wup_c, cw_c, cb_c, wdn_c, fg)


def _constants():
    j = jnp.arange(CHUNK)
    tail = (j[:, None] >= j[None, :]).astype(BF16)
    half = jnp.concatenate([tail, jnp.ones((CHUNK, LANES), BF16)], axis=1)
    mcat = jnp.concatenate([half, half], axis=0)
    i = jnp.arange(MXU_N)
    gmat = jnp.where(i[:, None] // HEAD_DIM == i[None, :] // HEAD_DIM, 1.0 / HEAD_DIM, 0.0).astype(BF16)
    return mcat, gmat


def kernel(x, norm1_g, w_in, sgu_norm_g, sgu_w, sgu_b, pool_w, pool_scale, mix_norm_g, w_o, norm2_g, w_up, conv_w,
           conv_b, w_down, final_g):
    b, s, d = x.shape
    depth = w_in.shape[0]
    w_a = sgu_norm_g.shape[1]
    w_c = pool_scale.shape[1]
    w_b = (w_in.shape[2] - 2 * w_a - w_c) // 3
    d_ff = w_down.shape[1]
    fc = MXU_N
    n_fc = d_ff // fc
    assert d_ff % fc == 0 and s % ROW_TILE == 0 and w_a + w_b + w_c == d
    mcat, gmat = _constants()

    xf = x.reshape(b * s, d)
    for l in range(depth):
        a, q, k, v, p = _in_proj(xf, norm1_g[l][None], w_in[l].astype(BF16), 2 * w_a, w_b, w_c)
        yb = _attention(q.reshape(b, s, w_b), k.reshape(b, s, w_b), v.reshape(b, s, w_b), mcat)
        sb_full = jnp.repeat(sgu_b[l].T, HEAD_DIM, axis=1)
        pw_bd = jax.scipy.linalg.block_diag(*[pool_w[l, g] for g in range(pool_w.shape[1])]).astype(BF16)
        x1 = _mixer_out(xf, a, yb.reshape(b * s, w_b), p, sgu_norm_g[l][None], sgu_w[l], sb_full, pw_bd,
                        pool_scale[l][None], mix_norm_g[l][None], w_o[l].astype(BF16), gmat, s)
        wup_c = w_up[l].reshape(d, 2, n_fc, fc).transpose(2, 0, 1, 3).reshape(n_fc, d, 2 * fc).astype(BF16)
        cw_c = conv_w[l].reshape(-1, 2, n_fc, fc).transpose(2, 0, 1, 3).reshape(n_fc, -1, 2 * fc)
        cb_c = conv_b[l].reshape(2, n_fc, fc).transpose(1, 0, 2).reshape(n_fc, 1, 2 * fc)
        wdn_c = w_down[l].reshape(n_fc, fc, d).astype(BF16)
        xf = _ffn(x1, norm2_g[l][None], wup_c, cw_c, cb_c, wdn_c, final_g[None], s, l == depth - 1)
    return xf.reshape(b, s, d)
```

```python
import functools

import jax
import jax.numpy as jnp
from jax import lax
from jax.experimental import pallas as pl
from jax.experimental.pallas import tpu as pltpu

F32 = jnp.float32
BF16 = jnp.bfloat16

EPS = 1e-6
HEAD_DIM = 64
CHUNK = 128
POOL_WINDOWS = (2, 4, 8, 16)
HALO = 16
LANES = 128
MXU_N = 256
ROW_TILE = 512
ATTN_UNITS = 4
VMEM_LIMIT = 56 * 1024 * 1024
LOG_WEIGHT_FLOOR = -104.0


def _split_bf16(x):
    hi = x.astype(BF16)
    lo = (x - hi.astype(F32)).astype(BF16)
    return hi, lo


def _row_rmsnorm(x, g):
    return x * lax.rsqrt(jnp.mean(x * x, axis=-1, keepdims=True) + EPS) * g


def _head_rmsnorm(y, g, gmat):
    y2 = y * y
    hi, lo = _split_bf16(y2)
    parts = []
    for c in range(y.shape[1] // MXU_N):
        sl = slice(c * MXU_N, (c + 1) * MXU_N)
        parts.append(jnp.dot(hi[:, sl], gmat, preferred_element_type=F32)
                     + jnp.dot(lo[:, sl], gmat, preferred_element_type=F32))
    ms = parts[0] if len(parts) == 1 else jnp.concatenate(parts, axis=1)
    return y * lax.rsqrt(ms + EPS) * g


def _const_spec(shape):
    return pl.BlockSpec(shape, lambda *_: (0,) * len(shape))


def _in_proj_kernel(x_ref, g_ref, w_ref, a_ref, q_ref, k_ref, v_ref, p_ref, *, w_a2, w_b):
    h = _row_rmsnorm(x_ref[...], g_ref[...]).astype(BF16)

    def proj(c0, c1):
        return jnp.dot(h, w_ref[:, c0:c1], preferred_element_type=F32)

    a_ref[...] = proj(0, w_a2)
    q_ref[...] = (proj(w_a2, w_a2 + w_b) * (HEAD_DIM ** -0.5)).astype(BF16)
    k_ref[...] = proj(w_a2 + w_b, w_a2 + 2 * w_b).astype(BF16)
    v_ref[...] = proj(w_a2 + 2 * w_b, w_a2 + 3 * w_b).astype(BF16)
    p_ref[...] = proj(w_a2 + 3 * w_b, w_ref.shape[1])


def _in_proj(x, g, w, w_a2, w_b, w_c):
    n, d = x.shape
    tm = ROW_TILE
    row = lambda c: pl.BlockSpec((tm, c), lambda i: (i, 0))
    return pl.pallas_call(
        functools.partial(_in_proj_kernel, w_a2=w_a2, w_b=w_b),
        grid=(n // tm,),
        in_specs=[row(d), _const_spec((1, d)), _const_spec(w.shape)],
        out_specs=[row(w_a2), row(w_b), row(w_b), row(w_b), row(w_c)],
        out_shape=[jax.ShapeDtypeStruct((n, w_a2), F32),
                   jax.ShapeDtypeStruct((n, w_b), BF16),
                   jax.ShapeDtypeStruct((n, w_b), BF16),
                   jax.ShapeDtypeStruct((n, w_b), BF16),
                   jax.ShapeDtypeStruct((n, w_c), F32)],
        compiler_params=pltpu.CompilerParams(dimension_semantics=("arbitrary",), vmem_limit_bytes=VMEM_LIMIT),
        name="in_proj",
    )(x, g, w)


def _attn_kernel(q_ref, k_ref, v_ref, m_ref, o_ref, qs_ref, z_ref, l_ref, att_ref, carry_ref, acc_ref,
                 *, n_heads, n_units):
    qi = pl.program_id(1)
    n_pairs = n_heads // 2
    pair_rows = 2 * CHUNK
    lane = lax.broadcasted_iota(jnp.int32, (pair_rows, LANES), 1)
    row = lax.broadcasted_iota(jnp.int32, (pair_rows, LANES), 0)
    own_lanes = (lane < HEAD_DIM) == (row < CHUNK)
    causal = lane < (row % CHUNK)
    first_head = lax.broadcasted_iota(jnp.int32, (CHUNK, LANES), 1) < HEAD_DIM
    nt_dims = (((1,), (1,)), ((), ()))
    sign_bit = jnp.uint32(0x80000000)
    all_units = tuple(range(n_units))

    def pair_slice(u, j):
        base = (u * n_pairs + j) * pair_rows
        return slice(base, base + pair_rows)

    def key_start(u, step):
        return pl.multiple_of((qi * n_units + u - step) * CHUNK, CHUNK)

    for u in all_units:
        for j in range(n_pairs):
            q2 = q_ref[0, u * CHUNK:(u + 1) * CHUNK, j * LANES:(j + 1) * LANES]
            q4 = jnp.concatenate([q2, q2], axis=0)
            qs_ref[pair_slice(u, j)] = jnp.where(own_lanes, q4, jnp.zeros_like(q4))
    acc_ref[...] = jnp.zeros_like(acc_ref)
    carry_ref[...] = jnp.zeros_like(carry_ref)

    def scores(units, step, diagonal):
        for u in units:
            k0 = key_start(u, step)
            for j in range(n_pairs):
                rows = pair_slice(u, j)
                k2 = k_ref[0, pl.ds(k0, CHUNK), j * LANES:(j + 1) * LANES]
                z = lax.dot_general(qs_ref[rows], k2, nt_dims, preferred_element_type=F32)
                neg_abs = lax.bitcast_convert_type(lax.bitcast_convert_type(z, jnp.uint32) | sign_bit, F32)
                softplus = jnp.maximum(z, 0.0) + jnp.log(1.0 + jnp.exp(neg_abs))
                if diagonal:
                    softplus = jnp.where(causal, softplus, 0.0)
                hi, lo = _split_bf16(softplus)
                z_ref[rows] = z
                l_ref[rows, :LANES] = hi
                l_ref[rows, LANES:] = lo

    def weights(units, diagonal):
        cmax = None
        for u in units:
            for j in range(n_pairs):
                rows = pair_slice(u, j)
                res = jnp.dot(l_ref[rows], m_ref[...], preferred_element_type=F32)
                c = carry_ref[rows]
                att = jnp.exp(z_ref[rows] + res[:, :LANES] + c)
                if diagonal:
                    att = jnp.where(causal, att, 0.0)
                att_ref[rows] = att.astype(BF16)
                c = c + res[:, LANES:]
                carry_ref[rows] = c
                cmax = c if cmax is None else jnp.maximum(cmax, c)
        return jnp.max(cmax)

    def values(units, step):
        for u in units:
            k0 = key_start(u, step)
            for j in range(n_pairs):
                cols = slice(j * LANES, (j + 1) * LANES)
                v2 = v_ref[0, pl.ds(k0, CHUNK), cols]
                o2 = jnp.dot(att_ref[pair_slice(u, j)], v2, preferred_element_type=F32)
                acc_ref[u * CHUNK:(u + 1) * CHUNK, cols] += jnp.where(first_head, o2[:CHUNK], o2[CHUNK:])

    scores(all_units, 0, True)
    cm = weights(all_units, True)

    def lockstep_cond(state):
        step, cm = state
        return jnp.logical_and(qi * n_units - step >= 0, cm > LOG_WEIGHT_FLOOR)

    def lockstep_body(state):
        step, _ = state
        scores(all_units, step, False)
        values(all_units, step - 1)
        return step + 1, weights(all_units, False)

    step, cm = lax.while_loop(lockstep_cond, lockstep_body, (1, cm))
    values(all_units, step - 1)

    for u in all_units[1:]:
        def cond(state, u=u):
            step, cm = state
            return jnp.logical_and(qi * n_units + u - step >= 0, cm > LOG_WEIGHT_FLOOR)

        def body(state, u=u):
            step, _ = state
            scores((u,), step, False)
            cm = weights((u,), False)
            values((u,), step)
            return step + 1, cm

        lax.while_loop(cond, body, (step, cm))
    o_ref[0] = acc_ref[...]


def _attention(q, k, v, mcat):
    b, s, w = q.shape
    n_heads = w // HEAD_DIM
    n_units = ATTN_UNITS
    tq = n_units * CHUNK
    rows = n_units * n_heads * CHUNK
    return pl.pallas_call(
        functools.partial(_attn_kernel, n_heads=n_heads, n_units=n_units),
        grid=(b, s // tq),
        in_specs=[pl.BlockSpec((1, tq, w), lambda bi, qi: (bi, qi, 0)),
                  pl.BlockSpec((1, s, w), lambda bi, qi: (bi, 0, 0)),
                  pl.BlockSpec((1, s, w), lambda bi, qi: (bi, 0, 0)),
                  _const_spec(mcat.shape)],
        out_specs=pl.BlockSpec((1, tq, w), lambda bi, qi: (bi, qi, 0)),
        out_shape=jax.ShapeDtypeStruct((b, s, w), F32),
        scratch_shapes=[pltpu.VMEM((rows, LANES), BF16),
                        pltpu.VMEM((rows, LANES), F32),
                        pltpu.VMEM((rows, 2 * LANES), BF16),
                        pltpu.VMEM((rows, LANES), BF16),
                        pltpu.VMEM((rows, LANES), F32),
                        pltpu.VMEM((tq, w), F32)],
        compiler_params=pltpu.CompilerParams(dimension_semantics=("arbitrary", "arbitrary"),
                                             vmem_limit_bytes=VMEM_LIMIT),
        name="attention",
    )(q, k, v, mcat)


def _mixer_out_kernel(x_ref, a_ref, yb_ref, p_ref, pprev_ref, sg_ref, sw_ref, sb_ref, pw_ref, ps_ref,
                      mg_ref, wo_ref, gm_ref, o_ref, *, tiles_per_seq):
    i = pl.program_id(0)
    tm = x_ref.shape[0]
    w_a = a_ref.shape[1] // 2
    w_c = p_ref.shape[1]
    gmat = gm_ref[...]
    seq_tile = i % tiles_per_seq

    a = a_ref[...]
    ga = 0.5 * a * (1.0 + lax.erf(a * (2.0 ** -0.5)))
    u = ga[:, :w_a]
    vn = _head_rmsnorm(ga[:, w_a:], sg_ref[...], gmat).astype(BF16)
    n_h = w_a // HEAD_DIM
    r = lax.broadcasted_iota(jnp.int32, (CHUNK, CHUNK), 0)
    cidx = lax.broadcasted_iota(jnp.int32, (CHUNK, CHUNK), 1)
    wms = [jnp.where(cidx <= r, sw_ref[h], 0.0).astype(BF16) for h in range(n_h)]
    head_of_lane = lax.broadcasted_iota(jnp.int32, (CHUNK, w_a), 1) // HEAD_DIM
    ya_parts = []
    for c in range(tm // CHUNK):
        vb = vn[c * CHUNK:(c + 1) * CHUNK]
        s = sb_ref[...]
        for h in range(n_h):
            sh = jnp.dot(wms[h], vb, preferred_element_type=F32)
            s = s + jnp.where(head_of_lane == h, sh, 0.0)
        ya_parts.append(u[c * CHUNK:(c + 1) * CHUNK] * s)
    ya = jnp.concatenate(ya_parts, axis=0)

    p = p_ref[...]
    pprev = jnp.where(seq_tile == 0, 0.0, pprev_ref[...])
    acc = jnp.concatenate([pprev, p], axis=0)
    group = lax.broadcasted_iota(jnp.int32, (tm, w_c), 1) // (w_c // len(POOL_WINDOWS))
    pos1 = (seq_tile * tm + lax.broadcasted_iota(jnp.int32, (tm, w_c), 0) + 1).astype(F32)
    win = jnp.zeros((tm, w_c), F32)
    cnt = jnp.ones((tm, w_c), F32)
    span = 1
    for g, wdw in enumerate(POOL_WINDOWS):
        while span < wdw:
            acc = acc + pltpu.roll(acc, span, 0)
            span *= 2
        win = jnp.where(group == g, acc[HALO:], win)
        cnt = jnp.where(group == g, jnp.minimum(pos1, float(wdw)), cnt)
    d = win / cnt - p
    yc = jnp.dot(d.astype(BF16), pw_ref[...], preferred_element_type=F32) * ps_ref[...]

    y = jnp.concatenate([ya, yb_ref[...], yc], axis=1)
    yn = _head_rmsnorm(y, mg_ref[...], gmat).astype(BF16)
    o_ref[...] = x_ref[...] + jnp.dot(yn, wo_ref[...], preferred_element_type=F32)


def _mixer_out(x, a, yb, p, sg, sw, sb_full, pw_bd, ps, mg, wo, gmat, seq_len):
    n, d = x.shape
    tm = ROW_TILE
    hb = tm // HALO
    row = lambda c: pl.BlockSpec((tm, c), lambda i: (i, 0))
    prev = pl.BlockSpec((HALO, p.shape[1]), lambda i: (jnp.maximum(i * hb - 1, 0), 0))
    return pl.pallas_call(
        functools.partial(_mixer_out_kernel, tiles_per_seq=seq_len // tm),
        grid=(n // tm,),
        in_specs=[row(d), row(a.shape[1]), row(yb.shape[1]), row(p.shape[1]), prev,
                  _const_spec(sg.shape), _const_spec(sw.shape), _const_spec(sb_full.shape),
                  _const_spec(pw_bd.shape), _const_spec(ps.shape), _const_spec(mg.shape),
                  _const_spec(wo.shape), _const_spec(gmat.shape)],
        out_specs=row(d),
        out_shape=jax.ShapeDtypeStruct((n, d), F32),
        compiler_params=pltpu.CompilerParams(dimension_semantics=("arbitrary",), vmem_limit_bytes=VMEM_LIMIT),
        name="mixer_out",
    )(x, a, yb, p, p, sg, sw, sb_full, pw_bd, ps, mg, wo, gmat)


def _ffn_kernel(x_ref, xprev_ref, g_ref, wup_ref, cw_ref, cb_ref, wdn_ref, fg_ref, o_ref, hext_ref, z_ref, act_ref,
                *, tiles_per_seq, final_norm):
    i = pl.program_id(0)
    tm = x_ref.shape[0]
    n_fc = wup_ref.shape[0]
    fc = wup_ref.shape[2] // 2
    h = _row_rmsnorm(x_ref[...], g_ref[...])
    hprev = _row_rmsnorm(xprev_ref[...], g_ref[...])
    hprev = jnp.where(i % tiles_per_seq == 0, 0.0, hprev)
    hext_ref[...] = jnp.concatenate([hprev, h], axis=0).astype(BF16)

    for c in range(n_fc):
        zb = z_ref.at[c % 2]
        zb[...] = jnp.dot(hext_ref[...], wup_ref[c], preferred_element_type=F32)
        cw = cw_ref[c]
        zc = (cb_ref[c] + zb[HALO - 2:HALO - 2 + tm] * cw[0:1] + zb[HALO - 1:HALO - 1 + tm] * cw[1:2]
              + zb[HALO:] * cw[2:3])
        gate = zc[:, :fc]
        act_ref[:, c * fc:(c + 1) * fc] = (gate * jax.nn.sigmoid(gate) * zc[:, fc:]).astype(BF16)

    out = x_ref[...] + jnp.dot(act_ref[...], wdn_ref[...], preferred_element_type=F32)
    if final_norm:
        out = _row_rmsnorm(out, fg_ref[...])
    o_ref[...] = out


def _ffn(x, g, wup_c, cw_c, cb_c, wdn, fg, seq_len, final_norm):
    n, d = x.shape
    tm = ROW_TILE
    hb = tm // HALO
    row = pl.BlockSpec((tm, d), lambda i: (i, 0))
    prev = pl.BlockSpec((HALO, d), lambda i: (jnp.maximum(i * hb - 1, 0), 0))
    return pl.pallas_call(
        functools.partial(_ffn_kernel, tiles_per_seq=seq_len // tm, final_norm=final_norm),
        grid=(n // tm,),
        in_specs=[row, prev, _const_spec(g.shape), _const_spec(wup_c.shape), _const_spec(cw_c.shape),
                  _const_spec(cb_c.shape), _const_spec(wdn.shape), _const_spec(fg.shape)],
        out_specs=row,
        out_shape=jax.ShapeDtypeStruct((n, d), F32),
        scratch_shapes=[pltpu.VMEM((HALO + tm, d), BF16),
                        pltpu.VMEM((2, HALO + tm, wup_c.shape[2]), F32),
                        pltpu.VMEM((tm, wdn.shape[0]), BF16)],
        compiler_params=pltpu.CompilerParams(dimension_semantics=("arbitrary",), vmem_limit_bytes=VMEM_LIMIT),
        name="ffn",
    )(x, x, g, wup_c, cw_c, cb_c, wdn, fg)


def _constants():
    j = jnp.arange(CHUNK)
    tail = (j[:, None] >= j[None, :]).astype(BF16)
    half = -jnp.concatenate([tail, jnp.ones((CHUNK, LANES), BF16)], axis=1)
    mcat = jnp.concatenate([half, half], axis=0)
    i = jnp.arange(MXU_N)
    gmat = jnp.where(i[:, None] // HEAD_DIM == i[None, :] // HEAD_DIM, 1.0 / HEAD_DIM, 0.0).astype(BF16)
    return mcat, gmat


def kernel(x, norm1_g, w_in, sgu_norm_g, sgu_w, sgu_b, pool_w, pool_scale, mix_norm_g, w_o, norm2_g, w_up, conv_w,
           conv_b, w_down, final_g):
    b, s, d = x.shape
    depth = w_in.shape[0]
    w_a = sgu_norm_g.shape[1]
    w_c = pool_scale.shape[1]
    w_b = (w_in.shape[2] - 2 * w_a - w_c) // 3
    d_ff = w_down.shape[1]
    fc = MXU_N
    n_fc = d_ff // fc
    assert d_ff % fc == 0 and s % ROW_TILE == 0 and w_a + w_b + w_c == d
    mcat, gmat = _constants()

    xf = x.reshape(b * s, d)
    for l in range(depth):
        a, q, k, v, p = _in_proj(xf, norm1_g[l][None], w_in[l].astype(BF16), 2 * w_a, w_b, w_c)
        yb = _attention(q.reshape(b, s, w_b), k.reshape(b, s, w_b), v.reshape(b, s, w_b), mcat)
        sb_full = jnp.repeat(sgu_b[l].T, HEAD_DIM, axis=1)
        pw_bd = jax.scipy.linalg.block_diag(*[pool_w[l, g] for g in range(pool_w.shape[1])]).astype(BF16)
        x1 = _mixer_out(xf, a, yb.reshape(b * s, w_b), p, sgu_norm_g[l][None], sgu_w[l], sb_full, pw_bd,
                        pool_scale[l][None], mix_norm_g[l][None], w_o[l].astype(BF16), gmat, s)
        wup_c = w_up[l].reshape(d, 2, n_fc, fc).transpose(2, 0, 1, 3).reshape(n_fc, d, 2 * fc).astype(BF16)
        cw_c = conv_w[l].reshape(-1, 2, n_fc, fc).transpose(2, 0, 1, 3).reshape(n_fc, -1, 2 * fc)
        cb_c = conv_b[l].reshape(2, n_fc, fc).transpose(1, 0, 2).reshape(n_fc, 1, 2 * fc)
        xf = _ffn(x1, norm2_g[l][None], wup_c, cw_c, cb_c, w_down[l].astype(BF16), final_g[None], s, l == depth - 1)
    return xf.reshape(b, s, d)
```

```python
import functools

import jax
import jax.numpy as jnp
from jax import lax
from jax.experimental import pallas as pl
from jax.experimental.pallas import tpu as pltpu

F32 = jnp.float32
BF16 = jnp.bfloat16

EPS = 1e-6
HEAD_DIM = 64
CHUNK = 128
POOL_WINDOWS = (2, 4, 8, 16)
HALO = 16
LANES = 128
MXU_N = 256
ROW_TILE = 512
ATTN_UNITS = 4
VMEM_LIMIT = 56 * 1024 * 1024
LOG_WEIGHT_FLOOR = -104.0


def _split_bf16(x):
    hi = x.astype(BF16)
    lo = (x - hi.astype(F32)).astype(BF16)
    return hi, lo


def _row_rmsnorm(x, g):
    return x * lax.rsqrt(jnp.mean(x * x, axis=-1, keepdims=True) + EPS) * g


def _head_rmsnorm(y, g, gmat):
    y2 = y * y
    hi, lo = _split_bf16(y2)
    parts = []
    for c in range(y.shape[1] // MXU_N):
        sl = slice(c * MXU_N, (c + 1) * MXU_N)
        parts.append(jnp.dot(hi[:, sl], gmat, preferred_element_type=F32)
                     + jnp.dot(lo[:, sl], gmat, preferred_element_type=F32))
    ms = parts[0] if len(parts) == 1 else jnp.concatenate(parts, axis=1)
    return y * lax.rsqrt(ms + EPS) * g


def _const_spec(shape):
    return pl.BlockSpec(shape, lambda *_: (0,) * len(shape))


def _in_proj_kernel(x_ref, g_ref, w_ref, a_ref, q_ref, k_ref, v_ref, p_ref, *, w_a2, w_b):
    h = _row_rmsnorm(x_ref[...], g_ref[...]).astype(BF16)

    def proj(c0, c1):
        return jnp.dot(h, w_ref[:, c0:c1], preferred_element_type=F32)

    a_ref[...] = proj(0, w_a2)
    q_ref[...] = (proj(w_a2, w_a2 + w_b) * (HEAD_DIM ** -0.5)).astype(BF16)
    k_ref[...] = proj(w_a2 + w_b, w_a2 + 2 * w_b).astype(BF16)
    v_ref[...] = proj(w_a2 + 2 * w_b, w_a2 + 3 * w_b).astype(BF16)
    p_ref[...] = proj(w_a2 + 3 * w_b, w_ref.shape[1])


def _in_proj(x, g, w, w_a2, w_b, w_c):
    n, d = x.shape
    tm = ROW_TILE
    row = lambda c: pl.BlockSpec((tm, c), lambda i: (i, 0))
    return pl.pallas_call(
        functools.partial(_in_proj_kernel, w_a2=w_a2, w_b=w_b),
        grid=(n // tm,),
        in_specs=[row(d), _const_spec((1, d)), _const_spec(w.shape)],
        out_specs=[row(w_a2), row(w_b), row(w_b), row(w_b), row(w_c)],
        out_shape=[jax.ShapeDtypeStruct((n, w_a2), F32),
                   jax.ShapeDtypeStruct((n, w_b), BF16),
                   jax.ShapeDtypeStruct((n, w_b), BF16),
                   jax.ShapeDtypeStruct((n, w_b), BF16),
                   jax.ShapeDtypeStruct((n, w_c), F32)],
        compiler_params=pltpu.CompilerParams(dimension_semantics=("arbitrary",), vmem_limit_bytes=VMEM_LIMIT),
        name="in_proj",
    )(x, g, w)


def _attn_kernel(q_ref, k_ref, v_ref, m_ref, o_ref, qs_ref, z_ref, l_ref, att_ref, carry_ref, acc_ref,
                 *, n_heads, n_units):
    qi = pl.program_id(1)
    n_pairs = n_heads // 2
    pair_rows = 2 * CHUNK
    lane = lax.broadcasted_iota(jnp.int32, (pair_rows, LANES), 1)
    row = lax.broadcasted_iota(jnp.int32, (pair_rows, LANES), 0)
    own_lanes = (lane < HEAD_DIM) == (row < CHUNK)
    causal = lane < (row % CHUNK)
    first_head = lax.broadcasted_iota(jnp.int32, (CHUNK, LANES), 1) < HEAD_DIM
    nt_dims = (((1,), (1,)), ((), ()))
    sign_bit = jnp.uint32(0x80000000)
    all_units = tuple(range(n_units))

    def pair_slice(u, j):
        base = (u * n_pairs + j) * pair_rows
        return slice(base, base + pair_rows)

    def key_start(u, step):
        return pl.multiple_of((qi * n_units + u - step) * CHUNK, CHUNK)

    for u in all_units:
        for j in range(n_pairs):
            q2 = q_ref[0, u * CHUNK:(u + 1) * CHUNK, j * LANES:(j + 1) * LANES]
            q4 = jnp.concatenate([q2, q2], axis=0)
            qs_ref[pair_slice(u, j)] = jnp.where(own_lanes, q4, jnp.zeros_like(q4))
    acc_ref[...] = jnp.zeros_like(acc_ref)
    carry_ref[...] = jnp.zeros_like(carry_ref)

    def scores(units, step, diagonal):
        for u in units:
            k0 = key_start(u, step)
            for j in range(n_pairs):
                rows = pair_slice(u, j)
                k2 = k_ref[0, pl.ds(k0, CHUNK), j * LANES:(j + 1) * LANES]
                z = lax.dot_general(qs_ref[rows], k2, nt_dims, preferred_element_type=F32)
                neg_abs = lax.bitcast_convert_type(lax.bitcast_convert_type(z, jnp.uint32) | sign_bit, F32)
                softplus = jnp.maximum(z, 0.0) + jnp.log(1.0 + jnp.exp(neg_abs))
                if diagonal:
                    softplus = jnp.where(causal, softplus, 0.0)
                z_ref[rows] = z
                l_ref[rows] = softplus.astype(BF16)

    def weights(units, diagonal):
        cmax = None
        for u in units:
            for j in range(n_pairs):
                rows = pair_slice(u, j)
                res = jnp.dot(l_ref[rows], m_ref[...], preferred_element_type=F32)
                c = carry_ref[rows]
                att = jnp.exp(z_ref[rows] + res[:, :LANES] + c)
                if diagonal:
                    att = jnp.where(causal, att, 0.0)
                att_ref[rows] = att.astype(BF16)
                c = c + res[:, LANES:]
                carry_ref[rows] = c
                cmax = c if cmax is None else jnp.maximum(cmax, c)
        return jnp.max(cmax)

    def values(units, step):
        for u in units:
            k0 = key_start(u, step)
            for j in range(n_pairs):
                cols = slice(j * LANES, (j + 1) * LANES)
                v2 = v_ref[0, pl.ds(k0, CHUNK), cols]
                o2 = jnp.dot(att_ref[pair_slice(u, j)], v2, preferred_element_type=F32)
                acc_ref[u * CHUNK:(u + 1) * CHUNK, cols] += jnp.where(first_head, o2[:CHUNK], o2[CHUNK:])

    scores(all_units, 0, True)
    cm = weights(all_units, True)

    def lockstep_cond(state):
        step, cm = state
        return jnp.logical_and(qi * n_units - step >= 0, cm > LOG_WEIGHT_FLOOR)

    def lockstep_body(state):
        step, _ = state
        scores(all_units, step, False)
        values(all_units, step - 1)
        return step + 1, weights(all_units, False)

    step, cm = lax.while_loop(lockstep_cond, lockstep_body, (1, cm))
    values(all_units, step - 1)

    for u in all_units[1:]:
        def cond(state, u=u):
            step, cm = state
            return jnp.logical_and(qi * n_units + u - step >= 0, cm > LOG_WEIGHT_FLOOR)

        def body(state, u=u):
            step, _ = state
            scores((u,), step, False)
            cm = weights((u,), False)
            values((u,), step)
            return step + 1, cm

        lax.while_loop(cond, body, (step, cm))
    o_ref[0] = acc_ref[...]


def _attention(q, k, v, mcat):
    b, s, w = q.shape
    n_heads = w // HEAD_DIM
    n_units = ATTN_UNITS
    tq = n_units * CHUNK
    rows = n_units * n_heads * CHUNK
    return pl.pallas_call(
        functools.partial(_attn_kernel, n_heads=n_heads, n_units=n_units),
        grid=(b, s // tq),
        in_specs=[pl.BlockSpec((1, tq, w), lambda bi, qi: (bi, qi, 0)),
                  pl.BlockSpec((1, s, w), lambda bi, qi: (bi, 0, 0)),
                  pl.BlockSpec((1, s, w), lambda bi, qi: (bi, 0, 0)),
                  _const_spec(mcat.shape)],
        out_specs=pl.BlockSpec((1, tq, w), lambda bi, qi: (bi, qi, 0)),
        out_shape=jax.ShapeDtypeStruct((b, s, w), F32),
        scratch_shapes=[pltpu.VMEM((rows, LANES), BF16),
                        pltpu.VMEM((rows, LANES), F32),
                        pltpu.VMEM((rows, LANES), BF16),
                        pltpu.VMEM((rows, LANES), BF16),
                        pltpu.VMEM((rows, LANES), F32),
                        pltpu.VMEM((tq, w), F32)],
        compiler_params=pltpu.CompilerParams(dimension_semantics=("arbitrary", "arbitrary"),
                                             vmem_limit_bytes=VMEM_LIMIT),
        name="attention",
    )(q, k, v, mcat)


def _mixer_out_kernel(x_ref, a_ref, yb_ref, p_ref, pprev_ref, sg_ref, sw_ref, sb_ref, pw_ref, ps_ref,
                      mg_ref, wo_ref, gm_ref, o_ref, *, tiles_per_seq):
    i = pl.program_id(0)
    tm = x_ref.shape[0]
    w_a = a_ref.shape[1] // 2
    w_c = p_ref.shape[1]
    gmat = gm_ref[...]
    seq_tile = i % tiles_per_seq

    a = a_ref[...]
    ga = 0.5 * a * (1.0 + lax.erf(a * (2.0 ** -0.5)))
    u = ga[:, :w_a]
    vn = _head_rmsnorm(ga[:, w_a:], sg_ref[...], gmat).astype(BF16)
    n_h = w_a // HEAD_DIM
    r = lax.broadcasted_iota(jnp.int32, (CHUNK, CHUNK), 0)
    cidx = lax.broadcasted_iota(jnp.int32, (CHUNK, CHUNK), 1)
    wms = [jnp.where(cidx <= r, sw_ref[h], 0.0).astype(BF16) for h in range(n_h)]
    head_of_lane = lax.broadcasted_iota(jnp.int32, (CHUNK, w_a), 1) // HEAD_DIM
    ya_parts = []
    for c in range(tm // CHUNK):
        vb = vn[c * CHUNK:(c + 1) * CHUNK]
        s = sb_ref[...]
        for h in range(n_h):
            sh = jnp.dot(wms[h], vb, preferred_element_type=F32)
            s = s + jnp.where(head_of_lane == h, sh, 0.0)
        ya_parts.append(u[c * CHUNK:(c + 1) * CHUNK] * s)
    ya = jnp.concatenate(ya_parts, axis=0)

    p = p_ref[...]
    pprev = jnp.where(seq_tile == 0, 0.0, pprev_ref[...])
    acc = jnp.concatenate([pprev, p], axis=0)
    group = lax.broadcasted_iota(jnp.int32, (tm, w_c), 1) // (w_c // len(POOL_WINDOWS))
    pos1 = (seq_tile * tm + lax.broadcasted_iota(jnp.int32, (tm, w_c), 0) + 1).astype(F32)
    win = jnp.zeros((tm, w_c), F32)
    cnt = jnp.ones((tm, w_c), F32)
    span = 1
    for g, wdw in enumerate(POOL_WINDOWS):
        while span < wdw:
            acc = acc + pltpu.roll(acc, span, 0)
            span *= 2
        win = jnp.where(group == g, acc[HALO:], win)
        cnt = jnp.where(group == g, jnp.minimum(pos1, float(wdw)), cnt)
    d = win / cnt - p
    yc = jnp.dot(d.astype(BF16), pw_ref[...], preferred_element_type=F32) * ps_ref[...]

    y = jnp.concatenate([ya, yb_ref[...], yc], axis=1)
    yn = _head_rmsnorm(y, mg_ref[...], gmat).astype(BF16)
    o_ref[...] = x_ref[...] + jnp.dot(yn, wo_ref[...], preferred_element_type=F32)


def _mixer_out(x, a, yb, p, sg, sw, sb_full, pw_bd, ps, mg, wo, gmat, seq_len):
    n, d = x.shape
    tm = ROW_TILE
    hb = tm // HALO
    row = lambda c: pl.BlockSpec((tm, c), lambda i: (i, 0))
    prev = pl.BlockSpec((HALO, p.shape[1]), lambda i: (jnp.maximum(i * hb - 1, 0), 0))
    return pl.pallas_call(
        functools.partial(_mixer_out_kernel, tiles_per_seq=seq_len // tm),
        grid=(n // tm,),
        in_specs=[row(d), row(a.shape[1]), row(yb.shape[1]), row(p.shape[1]), prev,
                  _const_spec(sg.shape), _const_spec(sw.shape), _const_spec(sb_full.shape),
                  _const_spec(pw_bd.shape), _const_spec(ps.shape), _const_spec(mg.shape),
                  _const_spec(wo.shape), _const_spec(gmat.shape)],
        out_specs=row(d),
        out_shape=jax.ShapeDtypeStruct((n, d), F32),
        compiler_params=pltpu.CompilerParams(dimension_semantics=("arbitrary",), vmem_limit_bytes=VMEM_LIMIT),
        name="mixer_out",
    )(x, a, yb, p, p, sg, sw, sb_full, pw_bd, ps, mg, wo, gmat)


def _ffn_kernel(x_ref, xprev_ref, g_ref, wup_ref, cw_ref, cb_ref, wdn_ref, fg_ref, o_ref, hext_ref, z_ref, act_ref,
                *, tiles_per_seq, final_norm):
    i = pl.program_id(0)
    tm = x_ref.shape[0]
    n_fc = wup_ref.shape[0]
    fc = wup_ref.shape[2] // 2
    h = _row_rmsnorm(x_ref[...], g_ref[...])
    hprev = _row_rmsnorm(xprev_ref[...], g_ref[...])
    hprev = jnp.where(i % tiles_per_seq == 0, 0.0, hprev)
    hext_ref[...] = jnp.concatenate([hprev, h], axis=0).astype(BF16)

    def up_proj(c, buf):
        z_ref[buf] = jnp.dot(hext_ref[...], wup_ref[c], preferred_element_type=F32)

    def conv_gate(c, buf):
        zb = z_ref.at[buf]
        cw = cw_ref[c]
        zc = (cb_ref[c] + zb[HALO - 2:HALO - 2 + tm] * cw[0:1] + zb[HALO - 1:HALO - 1 + tm] * cw[1:2]
              + zb[HALO:] * cw[2:3])
        gate = zc[:, :fc]
        act_ref[c] = (gate * jax.nn.sigmoid(gate) * zc[:, fc:]).astype(BF16)

    up_proj(0, 0)
    for c in range(n_fc):
        if c + 1 < n_fc:
            up_proj(c + 1, (c + 1) % 2)
        conv_gate(c, c % 2)

    out = x_ref[...]
    for c in range(n_fc):
        out = out + jnp.dot(act_ref[c], wdn_ref[c * fc:(c + 1) * fc], preferred_element_type=F32)
    if final_norm:
        out = _row_rmsnorm(out, fg_ref[...])
    o_ref[...] = out


def _ffn(x, g, wup_c, cw_c, cb_c, wdn, fg, seq_len, final_norm):
    n, d = x.shape
    tm = ROW_TILE
    hb = tm // HALO
    row = pl.BlockSpec((tm, d), lambda i: (i, 0))
    prev = pl.BlockSpec((HALO, d), lambda i: (jnp.maximum(i * hb - 1, 0), 0))
    return pl.pallas_call(
        functools.partial(_ffn_kernel, tiles_per_seq=seq_len // tm, final_norm=final_norm),
        grid=(n // tm,),
        in_specs=[row, prev, _const_spec(g.shape), _const_spec(wup_c.shape), _const_spec(cw_c.shape),
                  _const_spec(cb_c.shape), _const_spec(wdn.shape), _const_spec(fg.shape)],
        out_specs=row,
        out_shape=jax.ShapeDtypeStruct((n, d), F32),
        scratch_shapes=[pltpu.VMEM((HALO + tm, d), BF16),
                        pltpu.VMEM((2, HALO + tm, wup_c.shape[2]), F32),
                        pltpu.VMEM((wup_c.shape[0], tm, wup_c.shape[2] // 2), BF16)],
        compiler_params=pltpu.CompilerParams(dimension_semantics=("arbitrary",), vmem_limit_bytes=VMEM_LIMIT),
        name="ffn",
    )(x, x, g, wup_c, cw_c, cb_c, wdn, fg)


def _constants():
    j = jnp.arange(CHUNK)
    tail = (j[:, None] >= j[None, :]).astype(BF16)
    mcat = -jnp.concatenate([tail, jnp.ones((CHUNK, LANES), BF16)], axis=1)
    i = jnp.arange(MXU_N)
    gmat = jnp.where(i[:, None] // HEAD_DIM == i[None, :] // HEAD_DIM, 1.0 / HEAD_DIM, 0.0).astype(BF16)
    return mcat, gmat


def kernel(x, norm1_g, w_in, sgu_norm_g, sgu_w, sgu_b, pool_w, pool_scale, mix_norm_g, w_o, norm2_g, w_up, conv_w,
           conv_b, w_down, final_g):
    b, s, d = x.shape
    depth = w_in.shape[0]
    w_a = sgu_norm_g.shape[1]
    w_c = pool_scale.shape[1]
    w_b = (w_in.shape[2] - 2 * w_a - w_c) // 3
    d_ff = w_down.shape[1]
    fc = MXU_N
    n_fc = d_ff // fc
    assert d_ff % fc == 0 and s % ROW_TILE == 0 and w_a + w_b + w_c == d
    mcat, gmat = _constants()

    xf = x.reshape(b * s, d)
    for l in range(depth):
        a, q, k, v, p = _in_proj(xf, norm1_g[l][None], w_in[l].astype(BF16), 2 * w_a, w_b, w_c)
        yb = _attention(q.reshape(b, s, w_b), k.reshape(b, s, w_b), v.reshape(b, s, w_b), mcat)
        sb_full = jnp.repeat(sgu_b[l].T, HEAD_DIM, axis=1)
        pw_bd = jax.scipy.linalg.block_diag(*[pool_w[l, g] for g in range(pool_w.shape[1])]).astype(BF16)
        x1 = _mixer_out(xf, a, yb.reshape(b * s, w_b), p, sgu_norm_g[l][None], sgu_w[l], sb_full, pw_bd,
                        pool_scale[l][None], mix_norm_g[l][None], w_o[l].astype(BF16), gmat, s)
        wup_c = w_up[l].reshape(d, 2, n_fc, fc).transpose(2, 0, 1, 3).reshape(n_fc, d, 2 * fc).astype(BF16)
        cw_c = conv_w[l].reshape(-1, 2, n_fc, fc).transpose(2, 0, 1, 3).reshape(n_fc, -1, 2 * fc)
        cb_c = conv_b[l].reshape(2, n_fc, fc).transpose(1, 0, 2).reshape(n_fc, 1, 2 * fc)
        xf = _ffn(x1, norm2_g[l][None], wup_c, cw_c, cb_c, w_down[l].astype(BF16), final_g[None], s, l == depth - 1)
    return xf.reshape(b, s, d)
```

```python
import functools
from typing import Any, NamedTuple

import jax
import jax.numpy as jnp
from jax import lax
from jax.experimental import pallas as pl
from jax.experimental.pallas import tpu as pltpu

F32 = jnp.float32
BF16 = jnp.bfloat16

EPS = 1e-6
HEAD_DIM = 64
CHUNK = 128
POOL_WINDOWS = (2, 4, 8, 16)
HALO = 16
LANES = 128
MXU_N = 256
ROW_TILE = 512
ATTN_UNITS = 4
NARROW_ROWS = 32
VMEM_LIMIT = 56 * 1024 * 1024
LOG_WEIGHT_FLOOR = -104.0


def _split_bf16(x):
    hi = x.astype(BF16)
    lo = (x - hi.astype(F32)).astype(BF16)
    return hi, lo


def _row_rmsnorm(x, g):
    return x * lax.rsqrt(jnp.mean(x * x, axis=-1, keepdims=True) + EPS) * g


def _head_rmsnorm(y, g, gmat):
    y2 = y * y
    hi, lo = _split_bf16(y2)
    parts = []
    for c in range(y.shape[1] // MXU_N):
        sl = slice(c * MXU_N, (c + 1) * MXU_N)
        parts.append(jnp.dot(hi[:, sl], gmat, preferred_element_type=F32)
                     + jnp.dot(lo[:, sl], gmat, preferred_element_type=F32))
    ms = parts[0] if len(parts) == 1 else jnp.concatenate(parts, axis=1)
    return y * lax.rsqrt(ms + EPS) * g


def _const_spec(shape):
    return pl.BlockSpec(shape, lambda *_: (0,) * len(shape))


def _in_proj_kernel(x_ref, g_ref, w_ref, a_ref, q_ref, k_ref, v_ref, p_ref, *, w_a2, w_b):
    h = _row_rmsnorm(x_ref[...], g_ref[...]).astype(BF16)

    def proj(c0, c1):
        return jnp.dot(h, w_ref[:, c0:c1], preferred_element_type=F32)

    a_ref[...] = proj(0, w_a2)
    q_ref[...] = (proj(w_a2, w_a2 + w_b) * (HEAD_DIM ** -0.5)).astype(BF16)
    k_ref[...] = proj(w_a2 + w_b, w_a2 + 2 * w_b).astype(BF16)
    v_ref[...] = proj(w_a2 + 2 * w_b, w_a2 + 3 * w_b).astype(BF16)
    p_ref[...] = proj(w_a2 + 3 * w_b, w_ref.shape[1])


def _in_proj(x, g, w, w_a2, w_b, w_c):
    n, d = x.shape
    tm = ROW_TILE
    row = lambda c: pl.BlockSpec((tm, c), lambda i: (i, 0))
    return pl.pallas_call(
        functools.partial(_in_proj_kernel, w_a2=w_a2, w_b=w_b),
        grid=(n // tm,),
        in_specs=[row(d), _const_spec((1, d)), _const_spec(w.shape)],
        out_specs=[row(w_a2), row(w_b), row(w_b), row(w_b), row(w_c)],
        out_shape=[jax.ShapeDtypeStruct((n, w_a2), F32),
                   jax.ShapeDtypeStruct((n, w_b), BF16),
                   jax.ShapeDtypeStruct((n, w_b), BF16),
                   jax.ShapeDtypeStruct((n, w_b), BF16),
                   jax.ShapeDtypeStruct((n, w_c), F32)],
        compiler_params=pltpu.CompilerParams(dimension_semantics=("arbitrary",), vmem_limit_bytes=VMEM_LIMIT),
        name="in_proj",
    )(x, g, w)


class _SweepBuffers(NamedTuple):
    qs: Any
    z: Any
    l: Any
    att: Any
    carry: Any
    head_rows: int


def _attn_kernel(q_ref, k_ref, v_ref, m_ref, o_ref, qs_f, z_f, l_f, att_f, carry_f, qs_n, z_n, l_n, att_n, carry_n,
                 acc_ref, *, n_heads, n_units):
    qi = pl.program_id(1)
    n_pairs = n_heads // 2
    full = _SweepBuffers(qs_f, z_f, l_f, att_f, carry_f, CHUNK)
    narrow = _SweepBuffers(qs_n, z_n, l_n, att_n, carry_n, NARROW_ROWS)
    nt_dims = (((1,), (1,)), ((), ()))
    sign_bit = jnp.uint32(0x80000000)
    all_units = tuple(range(n_units))
    lane = lax.broadcasted_iota(jnp.int32, (2 * CHUNK, LANES), 1)
    row = lax.broadcasted_iota(jnp.int32, (2 * CHUNK, LANES), 0)
    causal = lane < (row % CHUNK)

    def pair_slice(bufs, u, j):
        base = (u * n_pairs + j) * 2 * bufs.head_rows
        return slice(base, base + 2 * bufs.head_rows)

    def first_head(bufs):
        return lax.broadcasted_iota(jnp.int32, (bufs.head_rows, LANES), 1) < HEAD_DIM

    def key_start(u, step):
        return pl.multiple_of((qi * n_units + u - step) * CHUNK, CHUNK)

    for bufs in (full, narrow):
        hr = bufs.head_rows
        ln = lax.broadcasted_iota(jnp.int32, (2 * hr, LANES), 1)
        rw = lax.broadcasted_iota(jnp.int32, (2 * hr, LANES), 0)
        own_lanes = (ln < HEAD_DIM) == (rw < hr)
        for u in all_units:
            for j in range(n_pairs):
                q2 = q_ref[0, u * CHUNK:u * CHUNK + hr, j * LANES:(j + 1) * LANES]
                q4 = jnp.concatenate([q2, q2], axis=0)
                bufs.qs[pair_slice(bufs, u, j)] = jnp.where(own_lanes, q4, jnp.zeros_like(q4))
    acc_ref[...] = jnp.zeros_like(acc_ref)
    carry_f[...] = jnp.zeros_like(carry_f)

    def scores(bufs, units, step, diagonal):
        for u in units:
            k0 = key_start(u, step)
            for j in range(n_pairs):
                rows = pair_slice(bufs, u, j)
                k2 = k_ref[0, pl.ds(k0, CHUNK), j * LANES:(j + 1) * LANES]
                z = lax.dot_general(bufs.qs[rows], k2, nt_dims, preferred_element_type=F32)
                neg_abs = lax.bitcast_convert_type(lax.bitcast_convert_type(z, jnp.uint32) | sign_bit, F32)
                softplus = jnp.maximum(z, 0.0) + jnp.log(1.0 + jnp.exp(neg_abs))
                if diagonal:
                    softplus = jnp.where(causal, softplus, 0.0)
                bufs.z[rows] = z
                bufs.l[rows] = softplus.astype(BF16)

    def weights(bufs, units, diagonal):
        hr = bufs.head_rows
        first = rest = None
        for u in units:
            for j in range(n_pairs):
                rows = pair_slice(bufs, u, j)
                res = jnp.dot(bufs.l[rows], m_ref[...], preferred_element_type=F32)
                c = bufs.carry[rows]
                att = jnp.exp(bufs.z[rows] + res[:, :LANES] + c)
                if diagonal:
                    att = jnp.where(causal, att, 0.0)
                bufs.att[rows] = att.astype(BF16)
                c = c + res[:, LANES:]
                bufs.carry[rows] = c
                for h0 in (0, hr):
                    cf = c[h0:h0 + NARROW_ROWS]
                    first = cf if first is None else jnp.maximum(first, cf)
                    if hr > NARROW_ROWS:
                        cr = c[h0 + NARROW_ROWS:h0 + hr]
                        rest = cr if rest is None else jnp.maximum(rest, cr)
        return jnp.max(first), (None if rest is None else jnp.max(rest))

    def values(bufs, units, step):
        hr = bufs.head_rows
        for u in units:
            k0 = key_start(u, step)
            for j in range(n_pairs):
                cols = slice(j * LANES, (j + 1) * LANES)
                v2 = v_ref[0, pl.ds(k0, CHUNK), cols]
                o2 = jnp.dot(bufs.att[pair_slice(bufs, u, j)], v2, preferred_element_type=F32)
                acc_ref[u * CHUNK:u * CHUNK + hr, cols] += jnp.where(first_head(bufs), o2[:hr], o2[hr:])

    def move_narrow_carries(to_narrow):
        for u in all_units:
            for j in range(n_pairs):
                fb = pair_slice(full, u, j).start
                nb = pair_slice(narrow, u, j).start
                for h in range(2):
                    f_rows = slice(fb + h * CHUNK, fb + h * CHUNK + NARROW_ROWS)
                    n_rows = slice(nb + h * NARROW_ROWS, nb + (h + 1) * NARROW_ROWS)
                    if to_narrow:
                        carry_n[n_rows] = carry_f[f_rows]
                    else:
                        carry_f[f_rows] = carry_n[n_rows]

    def unit0_has_keys(step):
        return qi * n_units - step >= 0

    scores(full, all_units, 0, True)
    c_first, c_rest = weights(full, all_units, True)

    def full_cond(state):
        step, _, c_rest = state
        return jnp.logical_and(unit0_has_keys(step), c_rest > LOG_WEIGHT_FLOOR)

    def full_body(state):
        step = state[0]
        scores(full, all_units, step, False)
        values(full, all_units, step - 1)
        c_first, c_rest = weights(full, all_units, False)
        return step + 1, c_first, c_rest

    step, c_first, c_rest = lax.while_loop(full_cond, full_body, (1, c_first, c_rest))
    values(full, all_units, step - 1)

    move_narrow_carries(True)

    def narrow_cond(state):
        step, c_first = state
        return jnp.logical_and(unit0_has_keys(step), c_first > LOG_WEIGHT_FLOOR)

    def narrow_body(state):
        step = state[0]
        scores(narrow, all_units, step, False)
        c_first, _ = weights(narrow, all_units, False)
        values(narrow, all_units, step)
        return step + 1, c_first

    step, c_first = lax.while_loop(narrow_cond, narrow_body, (step, c_first))
    move_narrow_carries(False)

    cm = jnp.maximum(c_first, c_rest)
    for u in all_units[1:]:
        def cond(state, u=u):
            step, cm = state
            return jnp.logical_and(qi * n_units + u - step >= 0, cm > LOG_WEIGHT_FLOOR)

        def body(state, u=u):
            step = state[0]
            scores(full, (u,), step, False)
            c_first, c_rest = weights(full, (u,), False)
            values(full, (u,), step)
            return step + 1, jnp.maximum(c_first, c_rest)

        lax.while_loop(cond, body, (step, cm))
    o_ref[0] = acc_ref[...]


def _attention(q, k, v, mcat):
    b, s, w = q.shape
    n_heads = w // HEAD_DIM
    n_units = ATTN_UNITS
    tq = n_units * CHUNK

    def sweep_scratch(head_rows):
        rows = n_units * n_heads * head_rows
        return [pltpu.VMEM((rows, LANES), BF16),
                pltpu.VMEM((rows, LANES), F32),
                pltpu.VMEM((rows, LANES), BF16),
                pltpu.VMEM((rows, LANES), BF16),
                pltpu.VMEM((rows, LANES), F32)]

    return pl.pallas_call(
        functools.partial(_attn_kernel, n_heads=n_heads, n_units=n_units),
        grid=(b, s // tq),
        in_specs=[pl.BlockSpec((1, tq, w), lambda bi, qi: (bi, qi, 0)),
                  pl.BlockSpec((1, s, w), lambda bi, qi: (bi, 0, 0)),
                  pl.BlockSpec((1, s, w), lambda bi, qi: (bi, 0, 0)),
                  _const_spec(mcat.shape)],
        out_specs=pl.BlockSpec((1, tq, w), lambda bi, qi: (bi, qi, 0)),
        out_shape=jax.ShapeDtypeStruct((b, s, w), F32),
        scratch_shapes=sweep_scratch(CHUNK) + sweep_scratch(NARROW_ROWS) + [pltpu.VMEM((tq, w), F32)],
        compiler_params=pltpu.CompilerParams(dimension_semantics=("arbitrary", "arbitrary"),
                                             vmem_limit_bytes=VMEM_LIMIT),
        name="attention",
    )(q, k, v, mcat)


def _mixer_out_kernel(x_ref, a_ref, yb_ref, p_ref, pprev_ref, sg_ref, sw_ref, sb_ref, pw_ref, ps_ref,
                      mg_ref, wo_ref, gm_ref, o_ref, *, tiles_per_seq):
    i = pl.program_id(0)
    tm = x_ref.shape[0]
    w_a = a_ref.shape[1] // 2
    w_c = p_ref.shape[1]
    gmat = gm_ref[...]
    seq_tile = i % tiles_per_seq

    p = p_ref[...]
    pprev = jnp.where(seq_tile == 0, 0.0, pprev_ref[...])
    acc = jnp.concatenate([pprev, p], axis=0)
    group = lax.broadcasted_iota(jnp.int32, (tm, w_c), 1) // (w_c // len(POOL_WINDOWS))
    pos1 = (seq_tile * tm + lax.broadcasted_iota(jnp.int32, (tm, w_c), 0) + 1).astype(F32)
    win = jnp.zeros((tm, w_c), F32)
    cnt = jnp.ones((tm, w_c), F32)
    span = 1
    for g, wdw in enumerate(POOL_WINDOWS):
        while span < wdw:
            acc = acc + pltpu.roll(acc, span, 0)
            span *= 2
        win = jnp.where(group == g, acc[HALO:], win)
        cnt = jnp.where(group == g, jnp.minimum(pos1, float(wdw)), cnt)
    d = win / cnt - p
    yc = jnp.dot(d.astype(BF16), pw_ref[...], preferred_element_type=F32) * ps_ref[...]

    n_h = w_a // HEAD_DIM
    r = lax.broadcasted_iota(jnp.int32, (CHUNK, CHUNK), 0)
    cidx = lax.broadcasted_iota(jnp.int32, (CHUNK, CHUNK), 1)
    wms = [jnp.where(cidx <= r, sw_ref[h], 0.0).astype(BF16) for h in range(n_h)]
    head_of_lane = lax.broadcasted_iota(jnp.int32, (CHUNK, w_a), 1) // HEAD_DIM

    a = a_ref[...]
    ga = 0.5 * a * (1.0 + lax.erf(a * (2.0 ** -0.5)))
    u = ga[:, :w_a]
    vn = _head_rmsnorm(ga[:, w_a:], sg_ref[...], gmat).astype(BF16)
    ya_parts = []
    for c in range(tm // CHUNK):
        vb = vn[c * CHUNK:(c + 1) * CHUNK]
        s = sb_ref[...]
        for h in range(n_h):
            sh = jnp.dot(wms[h], vb, preferred_element_type=F32)
            s = s + jnp.where(head_of_lane == h, sh, 0.0)
        ya_parts.append(u[c * CHUNK:(c + 1) * CHUNK] * s)
    ya = jnp.concatenate(ya_parts, axis=0)

    y = jnp.concatenate([ya, yb_ref[...], yc], axis=1)
    yn = _head_rmsnorm(y, mg_ref[...], gmat).astype(BF16)
    o_ref[...] = x_ref[...] + jnp.dot(yn, wo_ref[...], preferred_element_type=F32)


def _mixer_out(x, a, yb, p, sg, sw, sb_full, pw_bd, ps, mg, wo, gmat, seq_len):
    n, d = x.shape
    tm = ROW_TILE
    hb = tm // HALO
    row = lambda c: pl.BlockSpec((tm, c), lambda i: (i, 0))
    prev = pl.BlockSpec((HALO, p.shape[1]), lambda i: (jnp.maximum(i * hb - 1, 0), 0))
    return pl.pallas_call(
        functools.partial(_mixer_out_kernel, tiles_per_seq=seq_len // tm),
        grid=(n // tm,),
        in_specs=[row(d), row(a.shape[1]), row(yb.shape[1]), row(p.shape[1]), prev,
                  _const_spec(sg.shape), _const_spec(sw.shape), _const_spec(sb_full.shape),
                  _const_spec(pw_bd.shape), _const_spec(ps.shape), _const_spec(mg.shape),
                  _const_spec(wo.shape), _const_spec(gmat.shape)],
        out_specs=row(d),
        out_shape=jax.ShapeDtypeStruct((n, d), F32),
        compiler_params=pltpu.CompilerParams(dimension_semantics=("arbitrary",), vmem_limit_bytes=VMEM_LIMIT),
        name="mixer_out",
    )(x, a, yb, p, p, sg, sw, sb_full, pw_bd, ps, mg, wo, gmat)


def _ffn_kernel(x_ref, xprev_ref, g_ref, wup_ref, cw_ref, cb_ref, wdn_ref, fg_ref, o_ref, hext_ref, z_ref, act_ref,
                *, tiles_per_seq, final_norm):
    i = pl.program_id(0)
    tm = x_ref.shape[0]
    n_fc = wup_ref.shape[0]
    fc = wup_ref.shape[2] // 2
    h = _row_rmsnorm(x_ref[...], g_ref[...])
    hprev = _row_rmsnorm(xprev_ref[...], g_ref[...])
    hprev = jnp.where(i % tiles_per_seq == 0, 0.0, hprev)
    hext_ref[...] = jnp.concatenate([hprev, h], axis=0).astype(BF16)

    def up_proj(c, buf):
        z_ref[buf] = jnp.dot(hext_ref[...], wup_ref[c], preferred_element_type=F32)

    def conv_gate(c, buf):
        zb = z_ref.at[buf]
        cw = cw_ref[c]
        zc = (cb_ref[c] + zb[HALO - 2:HALO - 2 + tm] * cw[0:1] + zb[HALO - 1:HALO - 1 + tm] * cw[1:2]
              + zb[HALO:] * cw[2:3])
        gate = zc[:, :fc]
        act_ref[c] = (gate * jax.nn.sigmoid(gate) * zc[:, fc:]).astype(BF16)

    up_proj(0, 0)
    for c in range(n_fc):
        if c + 1 < n_fc:
            up_proj(c + 1, (c + 1) % 2)
        conv_gate(c, c % 2)

    out = x_ref[...]
    for c in range(n_fc):
        out = out + jnp.dot(act_ref[c], wdn_ref[c * fc:(c + 1) * fc], preferred_element_type=F32)
    if final_norm:
        out = _row_rmsnorm(out, fg_ref[...])
    o_ref[...] = out


def _ffn(x, g, wup_c, cw_c, cb_c, wdn, fg, seq_len, final_norm):
    n, d = x.shape
    tm = ROW_TILE
    hb = tm // HALO
    row = pl.BlockSpec((tm, d), lambda i: (i, 0))
    prev = pl.BlockSpec((HALO, d), lambda i: (jnp.maximum(i * hb - 1, 0), 0))
    return pl.pallas_call(
        functools.partial(_ffn_kernel, tiles_per_seq=seq_len // tm, final_norm=final_norm),
        grid=(n // tm,),
        in_specs=[row, prev, _const_spec(g.shape), _const_spec(wup_c.shape), _const_spec(cw_c.shape),
                  _const_spec(cb_c.shape), _const_spec(wdn.shape), _const_spec(fg.shape)],
        out_specs=row,
        out_shape=jax.ShapeDtypeStruct((n, d), F32),
        scratch_shapes=[pltpu.VMEM((HALO + tm, d), BF16),
                        pltpu.VMEM((2, HALO + tm, wup_c.shape[2]), F32),
                        pltpu.VMEM((wup_c.shape[0], tm, wup_c.shape[2] // 2), BF16)],
        compiler_params=pltpu.CompilerParams(dimension_semantics=("arbitrary",), vmem_limit_bytes=VMEM_LIMIT),
        name="ffn",
    )(x, x, g, wup_c, cw_c, cb_c, wdn, fg)


def _constants():
    j = jnp.arange(CHUNK)
    tail = (j[:, None] >= j[None, :]).astype(BF16)
    mcat = -jnp.concatenate([tail, jnp.ones((CHUNK, LANES), BF16)], axis=1)
    i = jnp.arange(MXU_N)
    gmat = jnp.where(i[:, None] // HEAD_DIM == i[None, :] // HEAD_DIM, 1.0 / HEAD_DIM, 0.0).astype(BF16)
    return mcat, gmat


def kernel(x, norm1_g, w_in, sgu_norm_g, sgu_w, sgu_b, pool_w, pool_scale, mix_norm_g, w_o, norm2_g, w_up, conv_w,
           conv_b, w_down, final_g):
    b, s, d = x.shape
    depth = w_in.shape[0]
    w_a = sgu_norm_g.shape[1]
    w_c = pool_scale.shape[1]
    w_b = (w_in.shape[2] - 2 * w_a - w_c) // 3
    d_ff = w_down.shape[1]
    fc = MXU_N
    n_fc = d_ff // fc
    assert d_ff % fc == 0 and s % ROW_TILE == 0 and w_a + w_b + w_c == d
    mcat, gmat = _constants()

    xf = x.reshape(b * s, d)
    for l in range(depth):
        a, q, k, v, p = _in_proj(xf, norm1_g[l][None], w_in[l].astype(BF16), 2 * w_a, w_b, w_c)
        yb = _attention(q.reshape(b, s, w_b), k.reshape(b, s, w_b), v.reshape(b, s, w_b), mcat)
        sb_full = jnp.repeat(sgu_b[l].T, HEAD_DIM, axis=1)
        pw_bd = jax.scipy.linalg.block_diag(*[pool_w[l, g] for g in range(pool_w.shape[1])]).astype(BF16)
        x1 = _mixer_out(xf, a, yb.reshape(b * s, w_b), p, sgu_norm_g[l][None], sgu_w[l], sb_full, pw_bd,
                        pool_scale[l][None], mix_norm_g[l][None], w_o[l].astype(BF16), gmat, s)
        wup_c = w_up[l].reshape(d, 2, n_fc, fc).transpose(2, 0, 1, 3).reshape(n_fc, d, 2 * fc).astype(BF16)
        cw_c = conv_w[l].reshape(-1, 2, n_fc, fc).transpose(2, 0, 1, 3).reshape(n_fc, -1, 2 * fc)
        cb_c = conv_b[l].reshape(2, n_fc, fc).transpose(1, 0, 2).reshape(n_fc, 1, 2 * fc)
        xf = _ffn(x1, norm2_g[l][None], wup_c, cw_c, cb_c, w_down[l].astype(BF16), final_g[None], s, l == depth - 1)
    return xf.reshape(b, s, d)
```

```python
import functools
from typing import Any, NamedTuple

import jax
import jax.numpy as jnp
from jax import lax
from jax.experimental import pallas as pl
from jax.experimental.pallas import tpu as pltpu

F32 = jnp.float32
BF16 = jnp.bfloat16

EPS = 1e-6
HEAD_DIM = 64
CHUNK = 128
POOL_WINDOWS = (2, 4, 8, 16)
HALO = 16
LANES = 128
MXU_N = 256
FFN_CHUNK = MXU_N
ROW_TILE = 512
ATTN_UNITS = 4
NARROW_ROWS = 32
VMEM_LIMIT = 56 * 1024 * 1024
LOG_WEIGHT_FLOOR = -104.0


def _row_rmsnorm(x, g):
    return x * lax.rsqrt(jnp.mean(x * x, axis=-1, keepdims=True) + EPS) * g


def _head_rmsnorm(y, g, gmat):
    y2 = (y * y).astype(BF16)
    parts = []
    for c in range(y.shape[1] // MXU_N):
        parts.append(jnp.dot(y2[:, c * MXU_N:(c + 1) * MXU_N], gmat, preferred_element_type=F32))
    ms = parts[0] if len(parts) == 1 else jnp.concatenate(parts, axis=1)
    return y * lax.rsqrt(ms + EPS) * g


def _const_spec(shape):
    return pl.BlockSpec(shape, lambda *_: (0,) * len(shape))


def _in_proj_kernel(x_ref, g_ref, w_ref, a_ref, q_ref, k_ref, v_ref, p_ref, *, w_a2, w_b):
    h = _row_rmsnorm(x_ref[...], g_ref[...]).astype(BF16)

    def proj(c0, c1):
        return jnp.dot(h, w_ref[:, c0:c1], preferred_element_type=F32)

    a_ref[...] = proj(0, w_a2)
    q_ref[...] = (proj(w_a2, w_a2 + w_b) * (HEAD_DIM ** -0.5)).astype(BF16)
    k_ref[...] = proj(w_a2 + w_b, w_a2 + 2 * w_b).astype(BF16)
    v_ref[...] = proj(w_a2 + 2 * w_b, w_a2 + 3 * w_b).astype(BF16)
    p_ref[...] = proj(w_a2 + 3 * w_b, w_ref.shape[1])


def _in_proj(x, g, w, w_a2, w_b, w_c):
    n, d = x.shape
    tm = ROW_TILE
    row = lambda c: pl.BlockSpec((tm, c), lambda i: (i, 0))
    return pl.pallas_call(
        functools.partial(_in_proj_kernel, w_a2=w_a2, w_b=w_b),
        grid=(n // tm,),
        in_specs=[row(d), _const_spec((1, d)), _const_spec(w.shape)],
        out_specs=[row(w_a2), row(w_b), row(w_b), row(w_b), row(w_c)],
        out_shape=[jax.ShapeDtypeStruct((n, w_a2), F32),
                   jax.ShapeDtypeStruct((n, w_b), BF16),
                   jax.ShapeDtypeStruct((n, w_b), BF16),
                   jax.ShapeDtypeStruct((n, w_b), BF16),
                   jax.ShapeDtypeStruct((n, w_c), F32)],
        compiler_params=pltpu.CompilerParams(dimension_semantics=("arbitrary",), vmem_limit_bytes=VMEM_LIMIT),
        name="in_proj",
    )(x, g, w)


class _SweepBuffers(NamedTuple):
    qs: Any
    z: Any
    l: Any
    att: Any
    carry: Any
    head_rows: int


def _attn_kernel(q_ref, k_ref, v_ref, m_ref, o_ref, qs_f, z_f, l_f, att_f, carry_f, qs_n, z_n, l_n, att_n, carry_n,
                 acc_ref, *, n_heads, n_units):
    qi = pl.program_id(1)
    n_pairs = n_heads // 2
    full = _SweepBuffers(qs_f, z_f, l_f, att_f, carry_f, CHUNK)
    narrow = _SweepBuffers(qs_n, z_n, l_n, att_n, carry_n, NARROW_ROWS)
    nt_dims = (((1,), (1,)), ((), ()))
    sign_bit = jnp.uint32(0x80000000)
    all_units = tuple(range(n_units))
    lane = lax.broadcasted_iota(jnp.int32, (2 * CHUNK, LANES), 1)
    row = lax.broadcasted_iota(jnp.int32, (2 * CHUNK, LANES), 0)
    causal = lane < (row % CHUNK)

    def pair_slice(bufs, u, j):
        base = (u * n_pairs + j) * 2 * bufs.head_rows
        return slice(base, base + 2 * bufs.head_rows)

    def first_head(bufs):
        return lax.broadcasted_iota(jnp.int32, (bufs.head_rows, LANES), 1) < HEAD_DIM

    def key_start(u, step):
        return pl.multiple_of((qi * n_units + u - step) * CHUNK, CHUNK)

    for bufs in (full, narrow):
        hr = bufs.head_rows
        ln = lax.broadcasted_iota(jnp.int32, (2 * hr, LANES), 1)
        rw = lax.broadcasted_iota(jnp.int32, (2 * hr, LANES), 0)
        own_lanes = (ln < HEAD_DIM) == (rw < hr)
        for u in all_units:
            for j in range(n_pairs):
                q2 = q_ref[0, u * CHUNK:u * CHUNK + hr, j * LANES:(j + 1) * LANES]
                q4 = jnp.concatenate([q2, q2], axis=0)
                bufs.qs[pair_slice(bufs, u, j)] = jnp.where(own_lanes, q4, jnp.zeros_like(q4))
    acc_ref[...] = jnp.zeros_like(acc_ref)
    carry_f[...] = jnp.zeros_like(carry_f)

    def scores(bufs, units, step, diagonal):
        for u in units:
            k0 = key_start(u, step)
            for j in range(n_pairs):
                rows = pair_slice(bufs, u, j)
                k2 = k_ref[0, pl.ds(k0, CHUNK), j * LANES:(j + 1) * LANES]
                z = lax.dot_general(bufs.qs[rows], k2, nt_dims, preferred_element_type=F32)
                neg_abs = lax.bitcast_convert_type(lax.bitcast_convert_type(z, jnp.uint32) | sign_bit, F32)
                softplus = jnp.maximum(z, 0.0) + jnp.log(1.0 + jnp.exp(neg_abs))
                if diagonal:
                    softplus = jnp.where(causal, softplus, 0.0)
                bufs.z[rows] = z
                bufs.l[rows] = softplus.astype(BF16)

    def weights(bufs, units, diagonal):
        hr = bufs.head_rows
        first = rest = None
        for u in units:
            for j in range(n_pairs):
                rows = pair_slice(bufs, u, j)
                res = jnp.dot(bufs.l[rows], m_ref[...], preferred_element_type=F32)
                c = bufs.carry[rows]
                att = jnp.exp(bufs.z[rows] + res[:, :LANES] + c)
                if diagonal:
                    att = jnp.where(causal, att, 0.0)
                bufs.att[rows] = att.astype(BF16)
                c = c + res[:, LANES:]
                bufs.carry[rows] = c
                for h0 in (0, hr):
                    cf = c[h0:h0 + NARROW_ROWS]
                    first = cf if first is None else jnp.maximum(first, cf)
                    if hr > NARROW_ROWS:
                        cr = c[h0 + NARROW_ROWS:h0 + hr]
                        rest = cr if rest is None else jnp.maximum(rest, cr)
        return jnp.max(first), (None if rest is None else jnp.max(rest))

    def values(bufs, units, step):
        hr = bufs.head_rows
        for u in units:
            k0 = key_start(u, step)
            for j in range(n_pairs):
                cols = slice(j * LANES, (j + 1) * LANES)
                v2 = v_ref[0, pl.ds(k0, CHUNK), cols]
                o2 = jnp.dot(bufs.att[pair_slice(bufs, u, j)], v2, preferred_element_type=F32)
                acc_ref[u * CHUNK:u * CHUNK + hr, cols] += jnp.where(first_head(bufs), o2[:hr], o2[hr:])

    def move_narrow_carries(to_narrow):
        for u in all_units:
            for j in range(n_pairs):
                fb = pair_slice(full, u, j).start
                nb = pair_slice(narrow, u, j).start
                for h in range(2):
                    f_rows = slice(fb + h * CHUNK, fb + h * CHUNK + NARROW_ROWS)
                    n_rows = slice(nb + h * NARROW_ROWS, nb + (h + 1) * NARROW_ROWS)
                    if to_narrow:
                        carry_n[n_rows] = carry_f[f_rows]
                    else:
                        carry_f[f_rows] = carry_n[n_rows]

    def unit0_has_keys(step):
        return qi * n_units - step >= 0

    scores(full, all_units, 0, True)
    c_first, c_rest = weights(full, all_units, True)

    def full_cond(state):
        step, _, c_rest = state
        return jnp.logical_and(unit0_has_keys(step), c_rest > LOG_WEIGHT_FLOOR)

    def full_body(state):
        step = state[0]
        scores(full, all_units, step, False)
        values(full, all_units, step - 1)
        c_first, c_rest = weights(full, all_units, False)
        return step + 1, c_first, c_rest

    step, c_first, c_rest = lax.while_loop(full_cond, full_body, (1, c_first, c_rest))
    values(full, all_units, step - 1)

    move_narrow_carries(True)

    def narrow_cond(state):
        step, c_first = state
        return jnp.logical_and(unit0_has_keys(step), c_first > LOG_WEIGHT_FLOOR)

    def narrow_body(state):
        step = state[0]
        scores(narrow, all_units, step, False)
        c_first, _ = weights(narrow, all_units, False)
        values(narrow, all_units, step)
        return step + 1, c_first

    step, c_first = lax.while_loop(narrow_cond, narrow_body, (step, c_first))
    move_narrow_carries(False)

    cm = jnp.maximum(c_first, c_rest)
    for u in all_units[1:]:
        def cond(state, u=u):
            step, cm = state
            return jnp.logical_and(qi * n_units + u - step >= 0, cm > LOG_WEIGHT_FLOOR)

        def body(state, u=u):
            step = state[0]
            scores(full, (u,), step, False)
            c_first, c_rest = weights(full, (u,), False)
            values(full, (u,), step)
            return step + 1, jnp.maximum(c_first, c_rest)

        lax.while_loop(cond, body, (step, cm))
    o_ref[0] = acc_ref[...]


def _attention(q, k, v, mcat):
    b, s, w = q.shape
    n_heads = w // HEAD_DIM
    n_units = ATTN_UNITS
    tq = n_units * CHUNK

    def sweep_scratch(head_rows):
        rows = n_units * n_heads * head_rows
        return [pltpu.VMEM((rows, LANES), BF16),
                pltpu.VMEM((rows, LANES), F32),
                pltpu.VMEM((rows, LANES), BF16),
                pltpu.VMEM((rows, LANES), BF16),
                pltpu.VMEM((rows, LANES), F32)]

    return pl.pallas_call(
        functools.partial(_attn_kernel, n_heads=n_heads, n_units=n_units),
        grid=(b, s // tq),
        in_specs=[pl.BlockSpec((1, tq, w), lambda bi, qi: (bi, qi, 0)),
                  pl.BlockSpec((1, s, w), lambda bi, qi: (bi, 0, 0)),
                  pl.BlockSpec((1, s, w), lambda bi, qi: (bi, 0, 0)),
                  _const_spec(mcat.shape)],
        out_specs=pl.BlockSpec((1, tq, w), lambda bi, qi: (bi, qi, 0)),
        out_shape=jax.ShapeDtypeStruct((b, s, w), F32),
        scratch_shapes=sweep_scratch(CHUNK) + sweep_scratch(NARROW_ROWS) + [pltpu.VMEM((tq, w), F32)],
        compiler_params=pltpu.CompilerParams(dimension_semantics=("arbitrary", "arbitrary"),
                                             vmem_limit_bytes=VMEM_LIMIT),
        name="attention",
    )(q, k, v, mcat)


def _mixer_out_kernel(x_ref, a_ref, yb_ref, p_ref, pprev_ref, sg_ref, sw_ref, sb_ref, pw_ref, ps_ref,
                      mg_ref, wo_ref, gm_ref, o_ref, *, tiles_per_seq):
    i = pl.program_id(0)
    tm = x_ref.shape[0]
    w_a = a_ref.shape[1] // 2
    w_c = p_ref.shape[1]
    gmat = gm_ref[...]
    seq_tile = i % tiles_per_seq

    p = p_ref[...]
    pprev = jnp.where(seq_tile == 0, 0.0, pprev_ref[...])
    acc = jnp.concatenate([pprev, p], axis=0)
    group = lax.broadcasted_iota(jnp.int32, (tm, w_c), 1) // (w_c // len(POOL_WINDOWS))
    pos1 = (seq_tile * tm + lax.broadcasted_iota(jnp.int32, (tm, w_c), 0) + 1).astype(F32)
    win = jnp.zeros((tm, w_c), F32)
    cnt = jnp.ones((tm, w_c), F32)
    span = 1
    for g, wdw in enumerate(POOL_WINDOWS):
        while span < wdw:
            acc = acc + pltpu.roll(acc, span, 0)
            span *= 2
        win = jnp.where(group == g, acc[HALO:], win)
        cnt = jnp.where(group == g, jnp.minimum(pos1, float(wdw)), cnt)
    d = win / cnt - p
    yc = jnp.dot(d.astype(BF16), pw_ref[...], preferred_element_type=F32) * ps_ref[...]

    n_h = w_a // HEAD_DIM
    r = lax.broadcasted_iota(jnp.int32, (CHUNK, CHUNK), 0)
    cidx = lax.broadcasted_iota(jnp.int32, (CHUNK, CHUNK), 1)
    wm_all = jnp.concatenate([jnp.where(cidx <= r, sw_ref[h], 0.0) for h in range(n_h)], axis=1).astype(BF16)
    head_of_lane = lax.broadcasted_iota(jnp.int32, (CHUNK, w_a), 1) // HEAD_DIM

    a = a_ref[...]
    ga = 0.5 * a * (1.0 + lax.erf(a * (2.0 ** -0.5)))
    u = ga[:, :w_a]
    vn = _head_rmsnorm(ga[:, w_a:], sg_ref[...], gmat).astype(BF16)
    ya_parts = []
    for c in range(tm // CHUNK):
        vb = vn[c * CHUNK:(c + 1) * CHUNK]
        v_bd = jnp.concatenate([jnp.where(head_of_lane == h, vb, jnp.zeros_like(vb)) for h in range(n_h)], axis=0)
        s = sb_ref[...] + jnp.dot(wm_all, v_bd, preferred_element_type=F32)
        ya_parts.append(u[c * CHUNK:(c + 1) * CHUNK] * s)
    ya = jnp.concatenate(ya_parts, axis=0)

    y = jnp.concatenate([ya, yb_ref[...], yc], axis=1)
    yn = _head_rmsnorm(y, mg_ref[...], gmat).astype(BF16)
    o_ref[...] = x_ref[...] + jnp.dot(yn, wo_ref[...], preferred_element_type=F32)


def _mixer_out(x, a, yb, p, sg, sw, sb_full, pw_bd, ps, mg, wo, gmat, seq_len):
    n, d = x.shape
    tm = ROW_TILE
    hb = tm // HALO
    row = lambda c: pl.BlockSpec((tm, c), lambda i: (i, 0))
    prev = pl.BlockSpec((HALO, p.shape[1]), lambda i: (jnp.maximum(i * hb - 1, 0), 0))
    return pl.pallas_call(
        functools.partial(_mixer_out_kernel, tiles_per_seq=seq_len // tm),
        grid=(n // tm,),
        in_specs=[row(d), row(a.shape[1]), row(yb.shape[1]), row(p.shape[1]), prev,
                  _const_spec(sg.shape), _const_spec(sw.shape), _const_spec(sb_full.shape),
                  _const_spec(pw_bd.shape), _const_spec(ps.shape), _const_spec(mg.shape),
                  _const_spec(wo.shape), _const_spec(gmat.shape)],
        out_specs=row(d),
        out_shape=jax.ShapeDtypeStruct((n, d), F32),
        compiler_params=pltpu.CompilerParams(dimension_semantics=("arbitrary",), vmem_limit_bytes=VMEM_LIMIT),
        name="mixer_out",
    )(x, a, yb, p, p, sg, sw, sb_full, pw_bd, ps, mg, wo, gmat)


def _ffn_kernel(x_ref, xprev_ref, g_ref, wup_ref, cw_ref, cb_ref, wdn_ref, fg_ref, o_ref, hext_ref, z_ref, act_ref,
                *, tiles_per_seq, final_norm):
    i = pl.program_id(0)
    tm = x_ref.shape[0]
    d_ff = wdn_ref.shape[0]
    fc = FFN_CHUNK
    n_fc = d_ff // fc
    h = _row_rmsnorm(x_ref[...], g_ref[...])
    hprev = _row_rmsnorm(xprev_ref[...], g_ref[...])
    hprev = jnp.where(i % tiles_per_seq == 0, 0.0, hprev)
    hext_ref[...] = jnp.concatenate([hprev, h], axis=0).astype(BF16)

    def gate_up(ref, c):
        return jnp.concatenate([ref[:, c * fc:(c + 1) * fc], ref[:, d_ff + c * fc:d_ff + (c + 1) * fc]], axis=1)

    def up_proj(c, buf):
        z_ref[buf] = jnp.dot(hext_ref[...], gate_up(wup_ref, c), preferred_element_type=F32)

    def conv_gate(c, buf):
        zb = z_ref.at[buf]
        cw = gate_up(cw_ref, c)
        zc = (gate_up(cb_ref, c) + zb[HALO - 2:HALO - 2 + tm] * cw[0:1] + zb[HALO - 1:HALO - 1 + tm] * cw[1:2]
              + zb[HALO:] * cw[2:3])
        gate = zc[:, :fc]
        act_ref[c] = (gate * jax.nn.sigmoid(gate) * zc[:, fc:]).astype(BF16)

    up_proj(0, 0)
    for c in range(n_fc):
        if c + 1 < n_fc:
            up_proj(c + 1, (c + 1) % 2)
        conv_gate(c, c % 2)

    out = x_ref[...]
    for c in range(n_fc):
        out = out + jnp.dot(act_ref[c], wdn_ref[c * fc:(c + 1) * fc], preferred_element_type=F32)
    if final_norm:
        out = _row_rmsnorm(out, fg_ref[...])
    o_ref[...] = out


def _ffn(x, g, wup, cw, cb, wdn, fg, seq_len, final_norm):
    n, d = x.shape
    tm = ROW_TILE
    d_ff = wdn.shape[0]
    hb = tm // HALO
    row = pl.BlockSpec((tm, d), lambda i: (i, 0))
    prev = pl.BlockSpec((HALO, d), lambda i: (jnp.maximum(i * hb - 1, 0), 0))
    return pl.pallas_call(
        functools.partial(_ffn_kernel, tiles_per_seq=seq_len // tm, final_norm=final_norm),
        grid=(n // tm,),
        in_specs=[row, prev, _const_spec(g.shape), _const_spec(wup.shape), _const_spec(cw.shape),
                  _const_spec(cb.shape), _const_spec(wdn.shape), _const_spec(fg.shape)],
        out_specs=row,
        out_shape=jax.ShapeDtypeStruct((n, d), F32),
        scratch_shapes=[pltpu.VMEM((HALO + tm, d), BF16),
                        pltpu.VMEM((2, HALO + tm, 2 * FFN_CHUNK), F32),
                        pltpu.VMEM((d_ff // FFN_CHUNK, tm, FFN_CHUNK), BF16)],
        compiler_params=pltpu.CompilerParams(dimension_semantics=("arbitrary",), vmem_limit_bytes=VMEM_LIMIT),
        name="ffn",
    )(x, x, g, wup, cw, cb, wdn, fg)


def _constants():
    j = jnp.arange(CHUNK)
    tail = (j[:, None] >= j[None, :]).astype(BF16)
    mcat = -jnp.concatenate([tail, jnp.ones((CHUNK, LANES), BF16)], axis=1)
    i = jnp.arange(MXU_N)
    gmat = jnp.where(i[:, None] // HEAD_DIM == i[None, :] // HEAD_DIM, 1.0 / HEAD_DIM, 0.0).astype(BF16)
    return mcat, gmat


def kernel(x, norm1_g, w_in, sgu_norm_g, sgu_w, sgu_b, pool_w, pool_scale, mix_norm_g, w_o, norm2_g, w_up, conv_w,
           conv_b, w_down, final_g):
    b, s, d = x.shape
    depth = w_in.shape[0]
    w_a = sgu_norm_g.shape[1]
    w_c = pool_scale.shape[1]
    w_b = (w_in.shape[2] - 2 * w_a - w_c) // 3
    assert w_down.shape[1] % FFN_CHUNK == 0 and s % ROW_TILE == 0 and w_a + w_b + w_c == d
    mcat, gmat = _constants()

    xf = x.reshape(b * s, d)
    for l in range(depth):
        a, q, k, v, p = _in_proj(xf, norm1_g[l][None], w_in[l].astype(BF16), 2 * w_a, w_b, w_c)
        yb = _attention(q.reshape(b, s, w_b), k.reshape(b, s, w_b), v.reshape(b, s, w_b), mcat)
        sb_full = jnp.repeat(sgu_b[l].T, HEAD_DIM, axis=1)
        pw_bd = jax.scipy.linalg.block_diag(*[pool_w[l, g] for g in range(pool_w.shape[1])]).astype(BF16)
        x1 = _mixer_out(xf, a, yb.reshape(b * s, w_b), p, sgu_norm_g[l][None], sgu_w[l], sb_full, pw_bd,
                        pool_scale[l][None], mix_norm_g[l][None], w_o[l].astype(BF16), gmat, s)
        xf = _ffn(x1, norm2_g[l][None], w_up[l].astype(BF16), conv_w[l], conv_b[l][None], w_down[l].astype(BF16),
                  final_g[None], s, l == depth - 1)
    return xf.reshape(b, s, d)
```

```python
import functools
from typing import Any, NamedTuple

import jax
import jax.numpy as jnp
from jax import lax
from jax.experimental import pallas as pl
from jax.experimental.pallas import tpu as pltpu

F32 = jnp.float32
BF16 = jnp.bfloat16

EPS = 1e-6
HEAD_DIM = 64
CHUNK = 128
POOL_WINDOWS = (2, 4, 8, 16)
HALO = 16
LANES = 128
MXU_N = 256
FFN_CHUNK = MXU_N
ROW_TILE = 512
ATTN_UNITS = 4
NARROW_ROWS = 64
VMEM_LIMIT = 56 * 1024 * 1024
LOG_WEIGHT_FLOOR = -104.0


def _row_rmsnorm(x, g):
    return x * lax.rsqrt(jnp.mean(x * x, axis=-1, keepdims=True) + EPS) * g


def _head_rmsnorm(y, g, gmat):
    y2 = (y * y).astype(BF16)
    parts = []
    for c in range(y.shape[1] // MXU_N):
        parts.append(jnp.dot(y2[:, c * MXU_N:(c + 1) * MXU_N], gmat, preferred_element_type=F32))
    ms = parts[0] if len(parts) == 1 else jnp.concatenate(parts, axis=1)
    return y * lax.rsqrt(ms + EPS) * g


def _const_spec(shape):
    return pl.BlockSpec(shape, lambda *_: (0,) * len(shape))


def _in_proj_kernel(x_ref, g_ref, w_ref, a_ref, q_ref, k_ref, v_ref, p_ref, *, w_a2, w_b):
    h = _row_rmsnorm(x_ref[...], g_ref[...]).astype(BF16)

    def proj(c0, c1):
        return jnp.dot(h, w_ref[:, c0:c1], preferred_element_type=F32)

    a_ref[...] = proj(0, w_a2)
    q_ref[...] = (proj(w_a2, w_a2 + w_b) * (HEAD_DIM ** -0.5)).astype(BF16)
    k_ref[...] = proj(w_a2 + w_b, w_a2 + 2 * w_b).astype(BF16)
    v_ref[...] = proj(w_a2 + 2 * w_b, w_a2 + 3 * w_b).astype(BF16)
    p_ref[...] = proj(w_a2 + 3 * w_b, w_ref.shape[1])


def _in_proj(x, g, w, w_a2, w_b, w_c):
    n, d = x.shape
    tm = ROW_TILE
    row = lambda c: pl.BlockSpec((tm, c), lambda i: (i, 0))
    return pl.pallas_call(
        functools.partial(_in_proj_kernel, w_a2=w_a2, w_b=w_b),
        grid=(n // tm,),
        in_specs=[row(d), _const_spec((1, d)), _const_spec(w.shape)],
        out_specs=[row(w_a2), row(w_b), row(w_b), row(w_b), row(w_c)],
        out_shape=[jax.ShapeDtypeStruct((n, w_a2), F32),
                   jax.ShapeDtypeStruct((n, w_b), BF16),
                   jax.ShapeDtypeStruct((n, w_b), BF16),
                   jax.ShapeDtypeStruct((n, w_b), BF16),
                   jax.ShapeDtypeStruct((n, w_c), F32)],
        compiler_params=pltpu.CompilerParams(dimension_semantics=("arbitrary",), vmem_limit_bytes=VMEM_LIMIT),
        name="in_proj",
    )(x, g, w)


class _SweepBuffers(NamedTuple):
    qs: Any
    z: Any
    l: Any
    att: Any
    carry: Any
    head_rows: int


def _attn_kernel(q_ref, k_ref, v_ref, m_ref, o_ref, qs_f, z_f, l_f, att_f, carry_f, qs_n, z_n, l_n, att_n, carry_n,
                 acc_ref, *, n_heads, n_units):
    qi = pl.program_id(1)
    n_pairs = n_heads // 2
    full = _SweepBuffers(qs_f, z_f, l_f, att_f, carry_f, CHUNK)
    narrow = _SweepBuffers(qs_n, z_n, l_n, att_n, carry_n, NARROW_ROWS)
    nt_dims = (((1,), (1,)), ((), ()))
    sign_bit = jnp.uint32(0x80000000)
    all_units = tuple(range(n_units))
    lane = lax.broadcasted_iota(jnp.int32, (2 * CHUNK, LANES), 1)
    row = lax.broadcasted_iota(jnp.int32, (2 * CHUNK, LANES), 0)
    causal = lane < (row % CHUNK)

    def pair_slice(bufs, u, j):
        base = (u * n_pairs + j) * 2 * bufs.head_rows
        return slice(base, base + 2 * bufs.head_rows)

    def first_head(bufs):
        return lax.broadcasted_iota(jnp.int32, (bufs.head_rows, LANES), 1) < HEAD_DIM

    def key_start(u, step):
        return pl.multiple_of((qi * n_units + u - step) * CHUNK, CHUNK)

    for bufs in (full, narrow):
        hr = bufs.head_rows
        ln = lax.broadcasted_iota(jnp.int32, (2 * hr, LANES), 1)
        rw = lax.broadcasted_iota(jnp.int32, (2 * hr, LANES), 0)
        own_lanes = (ln < HEAD_DIM) == (rw < hr)
        for u in all_units:
            for j in range(n_pairs):
                q2 = q_ref[0, u * CHUNK:u * CHUNK + hr, j * LANES:(j + 1) * LANES]
                q4 = jnp.concatenate([q2, q2], axis=0)
                bufs.qs[pair_slice(bufs, u, j)] = jnp.where(own_lanes, q4, jnp.zeros_like(q4))
    acc_ref[...] = jnp.zeros_like(acc_ref)
    carry_f[...] = jnp.zeros_like(carry_f)

    def scores(bufs, units, step, diagonal):
        for u in units:
            k0 = key_start(u, step)
            for j in range(n_pairs):
                rows = pair_slice(bufs, u, j)
                k2 = k_ref[0, pl.ds(k0, CHUNK), j * LANES:(j + 1) * LANES]
                z = lax.dot_general(bufs.qs[rows], k2, nt_dims, preferred_element_type=F32)
                neg_abs = lax.bitcast_convert_type(lax.bitcast_convert_type(z, jnp.uint32) | sign_bit, F32)
                softplus = jnp.maximum(z, 0.0) + jnp.log(1.0 + jnp.exp(neg_abs))
                if diagonal:
                    softplus = jnp.where(causal, softplus, 0.0)
                bufs.z[rows] = z
                bufs.l[rows] = softplus.astype(BF16)

    def weights(bufs, units, diagonal):
        hr = bufs.head_rows
        first = rest = None
        for u in units:
            for j in range(n_pairs):
                rows = pair_slice(bufs, u, j)
                res = jnp.dot(bufs.l[rows], m_ref[...], preferred_element_type=F32)
                c = bufs.carry[rows]
                att = jnp.exp(bufs.z[rows] + res[:, :LANES] + c)
                if diagonal:
                    att = jnp.where(causal, att, 0.0)
                bufs.att[rows] = att.astype(BF16)
                c = c + res[:, LANES:]
                bufs.carry[rows] = c
                for h0 in (0, hr):
                    cf = c[h0:h0 + NARROW_ROWS]
                    first = cf if first is None else jnp.maximum(first, cf)
                    if hr > NARROW_ROWS:
                        cr = c[h0 + NARROW_ROWS:h0 + hr]
                        rest = cr if rest is None else jnp.maximum(rest, cr)
        return jnp.max(first), (None if rest is None else jnp.max(rest))

    def values(bufs, units, step):
        hr = bufs.head_rows
        for u in units:
            k0 = key_start(u, step)
            for j in range(n_pairs):
                cols = slice(j * LANES, (j + 1) * LANES)
                v2 = v_ref[0, pl.ds(k0, CHUNK), cols]
                o2 = jnp.dot(bufs.att[pair_slice(bufs, u, j)], v2, preferred_element_type=F32)
                acc_ref[u * CHUNK:u * CHUNK + hr, cols] += jnp.where(first_head(bufs), o2[:hr], o2[hr:])

    def move_narrow_carries(to_narrow):
        for u in all_units:
            for j in range(n_pairs):
                fb = pair_slice(full, u, j).start
                nb = pair_slice(narrow, u, j).start
                for h in range(2):
                    f_rows = slice(fb + h * CHUNK, fb + h * CHUNK + NARROW_ROWS)
                    n_rows = slice(nb + h * NARROW_ROWS, nb + (h + 1) * NARROW_ROWS)
                    if to_narrow:
                        carry_n[n_rows] = carry_f[f_rows]
                    else:
                        carry_f[f_rows] = carry_n[n_rows]

    def unit0_has_keys(step):
        return qi * n_units - step >= 0

    scores(full, all_units, 0, True)
    c_first, c_rest = weights(full, all_units, True)

    def full_cond(state):
        step, _, c_rest = state
        return jnp.logical_and(unit0_has_keys(step), c_rest > LOG_WEIGHT_FLOOR)

    def full_body(state):
        step = state[0]
        scores(full, all_units, step, False)
        values(full, all_units, step - 1)
        c_first, c_rest = weights(full, all_units, False)
        return step + 1, c_first, c_rest

    step, c_first, c_rest = lax.while_loop(full_cond, full_body, (1, c_first, c_rest))
    values(full, all_units, step - 1)

    move_narrow_carries(True)

    def narrow_cond(state):
        step, c_first = state
        return jnp.logical_and(unit0_has_keys(step), c_first > LOG_WEIGHT_FLOOR)

    def narrow_body(state):
        step = state[0]
        scores(narrow, all_units, step, False)
        c_first, _ = weights(narrow, all_units, False)
        values(narrow, all_units, step)
        return step + 1, c_first

    step, c_first = lax.while_loop(narrow_cond, narrow_body, (step, c_first))
    move_narrow_carries(False)

    cm = jnp.maximum(c_first, c_rest)
    for u in all_units[1:]:
        def cond(state, u=u):
            step, cm = state
            return jnp.logical_and(qi * n_units + u - step >= 0, cm > LOG_WEIGHT_FLOOR)

        def body(state, u=u):
            step = state[0]
            scores(full, (u,), step, False)
            c_first, c_rest = weights(full, (u,), False)
            values(full, (u,), step)
            return step + 1, jnp.maximum(c_first, c_rest)

        lax.while_loop(cond, body, (step, cm))
    o_ref[0] = acc_ref[...]


def _attention(q, k, v, mcat):
    b, s, w = q.shape
    n_heads = w // HEAD_DIM
    n_units = ATTN_UNITS
    tq = n_units * CHUNK

    def sweep_scratch(head_rows):
        rows = n_units * n_heads * head_rows
        return [pltpu.VMEM((rows, LANES), BF16),
                pltpu.VMEM((rows, LANES), F32),
                pltpu.VMEM((rows, LANES), BF16),
                pltpu.VMEM((rows, LANES), BF16),
                pltpu.VMEM((rows, LANES), F32)]

    return pl.pallas_call(
        functools.partial(_attn_kernel, n_heads=n_heads, n_units=n_units),
        grid=(b, s // tq),
        in_specs=[pl.BlockSpec((1, tq, w), lambda bi, qi: (bi, qi, 0)),
                  pl.BlockSpec((1, s, w), lambda bi, qi: (bi, 0, 0)),
                  pl.BlockSpec((1, s, w), lambda bi, qi: (bi, 0, 0)),
                  _const_spec(mcat.shape)],
        out_specs=pl.BlockSpec((1, tq, w), lambda bi, qi: (bi, qi, 0)),
        out_shape=jax.ShapeDtypeStruct((b, s, w), F32),
        scratch_shapes=sweep_scratch(CHUNK) + sweep_scratch(NARROW_ROWS) + [pltpu.VMEM((tq, w), F32)],
        compiler_params=pltpu.CompilerParams(dimension_semantics=("arbitrary", "arbitrary"),
                                             vmem_limit_bytes=VMEM_LIMIT),
        name="attention",
    )(q, k, v, mcat)


def _mixer_out_kernel(x_ref, a_ref, yb_ref, p_ref, pprev_ref, sg_ref, sw_ref, sb_ref, pw_ref, ps_ref,
                      mg_ref, wo_ref, gm_ref, o_ref, *, tiles_per_seq):
    i = pl.program_id(0)
    tm = x_ref.shape[0]
    w_a = a_ref.shape[1] // 2
    w_c = p_ref.shape[1]
    gmat = gm_ref[...]
    seq_tile = i % tiles_per_seq

    p = p_ref[...]
    pprev = jnp.where(seq_tile == 0, 0.0, pprev_ref[...])
    acc = jnp.concatenate([pprev, p], axis=0)
    group = lax.broadcasted_iota(jnp.int32, (tm, w_c), 1) // (w_c // len(POOL_WINDOWS))
    pos1 = (seq_tile * tm + lax.broadcasted_iota(jnp.int32, (tm, w_c), 0) + 1).astype(F32)
    win = jnp.zeros((tm, w_c), F32)
    cnt = jnp.ones((tm, w_c), F32)
    span = 1
    for g, wdw in enumerate(POOL_WINDOWS):
        while span < wdw:
            acc = acc + pltpu.roll(acc, span, 0)
            span *= 2
        win = jnp.where(group == g, acc[HALO:], win)
        cnt = jnp.where(group == g, jnp.minimum(pos1, float(wdw)), cnt)
    d = win / cnt - p
    yc = jnp.dot(d.astype(BF16), pw_ref[...], preferred_element_type=F32) * ps_ref[...]

    n_h = w_a // HEAD_DIM
    r = lax.broadcasted_iota(jnp.int32, (CHUNK, CHUNK), 0)
    cidx = lax.broadcasted_iota(jnp.int32, (CHUNK, CHUNK), 1)
    wm_all = jnp.concatenate([jnp.where(cidx <= r, sw_ref[h], 0.0) for h in range(n_h)], axis=1).astype(BF16)
    head_of_lane = lax.broadcasted_iota(jnp.int32, (CHUNK, w_a), 1) // HEAD_DIM

    a = a_ref[...]
    ga = 0.5 * a * (1.0 + lax.erf(a * (2.0 ** -0.5)))
    u = ga[:, :w_a]
    vn = _head_rmsnorm(ga[:, w_a:], sg_ref[...], gmat).astype(BF16)
    ya_parts = []
    for c in range(tm // CHUNK):
        vb = vn[c * CHUNK:(c + 1) * CHUNK]
        v_bd = jnp.concatenate([jnp.where(head_of_lane == h, vb, jnp.zeros_like(vb)) for h in range(n_h)], axis=0)
        s = sb_ref[...] + jnp.dot(wm_all, v_bd, preferred_element_type=F32)
        ya_parts.append(u[c * CHUNK:(c + 1) * CHUNK] * s)
    ya = jnp.concatenate(ya_parts, axis=0)

    y = jnp.concatenate([ya, yb_ref[...], yc], axis=1)
    yn = _head_rmsnorm(y, mg_ref[...], gmat).astype(BF16)
    o_ref[...] = x_ref[...] + jnp.dot(yn, wo_ref[...], preferred_element_type=F32)


def _mixer_out(x, a, yb, p, sg, sw, sb_full, pw_bd, ps, mg, wo, gmat, seq_len):
    n, d = x.shape
    tm = ROW_TILE
    hb = tm // HALO
    row = lambda c: pl.BlockSpec((tm, c), lambda i: (i, 0))
    prev = pl.BlockSpec((HALO, p.shape[1]), lambda i: (jnp.maximum(i * hb - 1, 0), 0))
    return pl.pallas_call(
        functools.partial(_mixer_out_kernel, tiles_per_seq=seq_len // tm),
        grid=(n // tm,),
        in_specs=[row(d), row(a.shape[1]), row(yb.shape[1]), row(p.shape[1]), prev,
                  _const_spec(sg.shape), _const_spec(sw.shape), _const_spec(sb_full.shape),
                  _const_spec(pw_bd.shape), _const_spec(ps.shape), _const_spec(mg.shape),
                  _const_spec(wo.shape), _const_spec(gmat.shape)],
        out_specs=row(d),
        out_shape=jax.ShapeDtypeStruct((n, d), F32),
        compiler_params=pltpu.CompilerParams(dimension_semantics=("arbitrary",), vmem_limit_bytes=VMEM_LIMIT),
        name="mixer_out",
    )(x, a, yb, p, p, sg, sw, sb_full, pw_bd, ps, mg, wo, gmat)


def _ffn_kernel(x_ref, xprev_ref, g_ref, wup_ref, cw_ref, cb_ref, wdn_ref, fg_ref, o_ref, hext_ref, z_ref, act_ref,
                *, tiles_per_seq, final_norm):
    i = pl.program_id(0)
    tm = x_ref.shape[0]
    d_ff = wdn_ref.shape[0]
    fc = FFN_CHUNK
    n_fc = d_ff // fc
    h = _row_rmsnorm(x_ref[...], g_ref[...])
    hprev = _row_rmsnorm(xprev_ref[...], g_ref[...])
    hprev = jnp.where(i % tiles_per_seq == 0, 0.0, hprev)
    hext_ref[...] = jnp.concatenate([hprev, h], axis=0).astype(BF16)

    def gate_up(ref, c):
        return jnp.concatenate([ref[:, c * fc:(c + 1) * fc], ref[:, d_ff + c * fc:d_ff + (c + 1) * fc]], axis=1)

    def up_proj(c, buf):
        z_ref[buf] = jnp.dot(hext_ref[...], gate_up(wup_ref, c), preferred_element_type=F32)

    def conv_gate(c, buf):
        zb = z_ref.at[buf]
        cw = gate_up(cw_ref, c)
        zc = (gate_up(cb_ref, c) + zb[HALO - 2:HALO - 2 + tm] * cw[0:1] + zb[HALO - 1:HALO - 1 + tm] * cw[1:2]
              + zb[HALO:] * cw[2:3])
        gate = zc[:, :fc]
        act_ref[:, c * fc:(c + 1) * fc] = (gate * jax.nn.sigmoid(gate) * zc[:, fc:]).astype(BF16)

    up_proj(0, 0)
    for c in range(n_fc):
        if c + 1 < n_fc:
            up_proj(c + 1, (c + 1) % 2)
        conv_gate(c, c % 2)

    out = x_ref[...]
    for k0 in range(0, d_ff, MXU_N):
        out = out + jnp.dot(act_ref[:, k0:k0 + MXU_N], wdn_ref[k0:k0 + MXU_N], preferred_element_type=F32)
    if final_norm:
        out = _row_rmsnorm(out, fg_ref[...])
    o_ref[...] = out


def _ffn(x, g, wup, cw, cb, wdn, fg, seq_len, final_norm):
    n, d = x.shape
    tm = ROW_TILE
    d_ff = wdn.shape[0]
    hb = tm // HALO
    row = pl.BlockSpec((tm, d), lambda i: (i, 0))
    prev = pl.BlockSpec((HALO, d), lambda i: (jnp.maximum(i * hb - 1, 0), 0))
    return pl.pallas_call(
        functools.partial(_ffn_kernel, tiles_per_seq=seq_len // tm, final_norm=final_norm),
        grid=(n // tm,),
        in_specs=[row, prev, _const_spec(g.shape), _const_spec(wup.shape), _const_spec(cw.shape),
                  _const_spec(cb.shape), _const_spec(wdn.shape), _const_spec(fg.shape)],
        out_specs=row,
        out_shape=jax.ShapeDtypeStruct((n, d), F32),
        scratch_shapes=[pltpu.VMEM((HALO + tm, d), BF16),
                        pltpu.VMEM((2, HALO + tm, 2 * FFN_CHUNK), F32),
                        pltpu.VMEM((tm, d_ff), BF16)],
        compiler_params=pltpu.CompilerParams(dimension_semantics=("arbitrary",), vmem_limit_bytes=VMEM_LIMIT),
        name="ffn",
    )(x, x, g, wup, cw, cb, wdn, fg)


def _constants():
    j = jnp.arange(CHUNK)
    tail = (j[:, None] >= j[None, :]).astype(BF16)
    mcat = -jnp.concatenate([tail, jnp.ones((CHUNK, LANES), BF16)], axis=1)
    i = jnp.arange(MXU_N)
    gmat = jnp.where(i[:, None] // HEAD_DIM == i[None, :] // HEAD_DIM, 1.0 / HEAD_DIM, 0.0).astype(BF16)
    return mcat, gmat


def kernel(x, norm1_g, w_in, sgu_norm_g, sgu_w, sgu_b, pool_w, pool_scale, mix_norm_g, w_o, norm2_g, w_up, conv_w,
           conv_b, w_down, final_g):
    b, s, d = x.shape
    depth = w_in.shape[0]
    w_a = sgu_norm_g.shape[1]
    w_c = pool_scale.shape[1]
    w_b = (w_in.shape[2] - 2 * w_a - w_c) // 3
    assert w_down.shape[1] % FFN_CHUNK == 0 and s % ROW_TILE == 0 and w_a + w_b + w_c == d
    mcat, gmat = _constants()

    xf = x.reshape(b * s, d)
    for l in range(depth):
        a, q, k, v, p = _in_proj(xf, norm1_g[l][None], w_in[l].astype(BF16), 2 * w_a, w_b, w_c)
        yb = _attention(q.reshape(b, s, w_b), k.reshape(b, s, w_b), v.reshape(b, s, w_b), mcat)
        sb_full = jnp.repeat(sgu_b[l].T, HEAD_DIM, axis=1)
        pw_bd = jax.scipy.linalg.block_diag(*[pool_w[l, g] for g in range(pool_w.shape[1])]).astype(BF16)
        x1 = _mixer_out(xf, a, yb.reshape(b * s, w_b), p, sgu_norm_g[l][None], sgu_w[l], sb_full, pw_bd,
                        pool_scale[l][None], mix_norm_g[l][None], w_o[l].astype(BF16), gmat, s)
        xf = _ffn(x1, norm2_g[l][None], w_up[l].astype(BF16), conv_w[l], conv_b[l][None], w_down[l].astype(BF16),
                  final_g[None], s, l == depth - 1)
    return xf.reshape(b, s, d)
```

```python
import functools
from typing import Any, NamedTuple

import jax
import jax.numpy as jnp
from jax import lax
from jax.experimental import pallas as pl
from jax.experimental.pallas import tpu as pltpu

F32 = jnp.float32
BF16 = jnp.bfloat16

EPS = 1e-6
HEAD_DIM = 64
CHUNK = 128
POOL_WINDOWS = (2, 4, 8, 16)
HALO = 16
LANES = 128
SUBLANES = 8
MXU_N = 256
FFN_CHUNK = MXU_N
ROW_TILE = 512
ATTN_UNITS = 4
NARROW_ROWS = 64
VMEM_LIMIT = 56 * 1024 * 1024
LOG_WEIGHT_FLOOR = -104.0
MASKED_SCORE = -1e30


def _row_rmsnorm(x, g):
    return x * lax.rsqrt(jnp.mean(x * x, axis=-1, keepdims=True) + EPS) * g


def _head_rmsnorm(y, g, gmat):
    y2 = (y * y).astype(BF16)
    parts = []
    for c in range(y.shape[1] // MXU_N):
        parts.append(jnp.dot(y2[:, c * MXU_N:(c + 1) * MXU_N], gmat, preferred_element_type=F32))
    ms = parts[0] if len(parts) == 1 else jnp.concatenate(parts, axis=1)
    return y * lax.rsqrt(ms + EPS) * g


def _const_spec(shape):
    return pl.BlockSpec(shape, lambda *_: (0,) * len(shape))


def _in_proj_kernel(x_ref, g_ref, w_ref, a_ref, q_ref, k_ref, v_ref, p_ref, *, w_a2, w_b):
    h = _row_rmsnorm(x_ref[...], g_ref[...]).astype(BF16)

    def proj(c0, c1):
        return jnp.dot(h, w_ref[:, c0:c1], preferred_element_type=F32)

    a_ref[...] = proj(0, w_a2)
    q_ref[...] = (proj(w_a2, w_a2 + w_b) * (HEAD_DIM ** -0.5)).astype(BF16)
    k_ref[...] = proj(w_a2 + w_b, w_a2 + 2 * w_b).astype(BF16)
    v_ref[...] = proj(w_a2 + 2 * w_b, w_a2 + 3 * w_b).astype(BF16)
    p_ref[...] = proj(w_a2 + 3 * w_b, w_ref.shape[1])


def _in_proj(x, g, w, w_a2, w_b, w_c):
    n, d = x.shape
    tm = ROW_TILE
    row = lambda c: pl.BlockSpec((tm, c), lambda i: (i, 0))
    return pl.pallas_call(
        functools.partial(_in_proj_kernel, w_a2=w_a2, w_b=w_b),
        grid=(n // tm,),
        in_specs=[row(d), _const_spec((1, d)), _const_spec(w.shape)],
        out_specs=[row(w_a2), row(w_b), row(w_b), row(w_b), row(w_c)],
        out_shape=[jax.ShapeDtypeStruct((n, w_a2), F32),
                   jax.ShapeDtypeStruct((n, w_b), BF16),
                   jax.ShapeDtypeStruct((n, w_b), BF16),
                   jax.ShapeDtypeStruct((n, w_b), BF16),
                   jax.ShapeDtypeStruct((n, w_c), F32)],
        compiler_params=pltpu.CompilerParams(dimension_semantics=("arbitrary",), vmem_limit_bytes=VMEM_LIMIT),
        name="in_proj",
    )(x, g, w)


class _SweepBuffers(NamedTuple):
    qs: Any
    z: Any
    l: Any
    att: Any
    carry: Any
    head_rows: int


def _attn_kernel(q_ref, k_ref, v_ref, m_ref, o_ref, qs_f, z_f, l_f, att_f, carry_f, qs_n, z_n, l_n, att_n, carry_n,
                 acc_ref, *, n_heads, n_units):
    qi = pl.program_id(1)
    n_pairs = n_heads // 2
    full = _SweepBuffers(qs_f, z_f, l_f, att_f, carry_f, CHUNK)
    narrow = _SweepBuffers(qs_n, z_n, l_n, att_n, carry_n, NARROW_ROWS)
    nt_dims = (((1,), (1,)), ((), ()))
    sign_bit = jnp.uint32(0x80000000)
    all_units = tuple(range(n_units))
    lane = lax.broadcasted_iota(jnp.int32, (2 * CHUNK, LANES), 1)
    row = lax.broadcasted_iota(jnp.int32, (2 * CHUNK, LANES), 0)
    non_causal_bias = jnp.where(lane < (row % CHUNK), 0.0, MASKED_SCORE)

    def pair_slice(bufs, u, j):
        base = (u * n_pairs + j) * 2 * bufs.head_rows
        return slice(base, base + 2 * bufs.head_rows)

    def first_head(bufs):
        return lax.broadcasted_iota(jnp.int32, (bufs.head_rows, LANES), 1) < HEAD_DIM

    def key_start(u, step):
        return pl.multiple_of((qi * n_units + u - step) * CHUNK, CHUNK)

    for bufs in (full, narrow):
        hr = bufs.head_rows
        ln = lax.broadcasted_iota(jnp.int32, (2 * hr, LANES), 1)
        rw = lax.broadcasted_iota(jnp.int32, (2 * hr, LANES), 0)
        own_lanes = (ln < HEAD_DIM) == (rw < hr)
        for u in all_units:
            for j in range(n_pairs):
                q2 = q_ref[0, u * CHUNK:u * CHUNK + hr, j * LANES:(j + 1) * LANES]
                q4 = jnp.concatenate([q2, q2], axis=0)
                bufs.qs[pair_slice(bufs, u, j)] = jnp.where(own_lanes, q4, jnp.zeros_like(q4))
    acc_ref[...] = jnp.zeros_like(acc_ref)
    carry_f[...] = jnp.zeros_like(carry_f)

    def scores(bufs, units, step, diagonal):
        for u in units:
            k0 = key_start(u, step)
            for j in range(n_pairs):
                rows = pair_slice(bufs, u, j)
                k2 = k_ref[0, pl.ds(k0, CHUNK), j * LANES:(j + 1) * LANES]
                z = lax.dot_general(bufs.qs[rows], k2, nt_dims, preferred_element_type=F32)
                if diagonal:
                    z = z + non_causal_bias
                neg_abs = lax.bitcast_convert_type(lax.bitcast_convert_type(z, jnp.uint32) | sign_bit, F32)
                softplus = jnp.maximum(z, 0.0) + jnp.log(1.0 + jnp.exp(neg_abs))
                bufs.z[rows] = z
                bufs.l[rows] = softplus.astype(BF16)

    def weights(bufs, units):
        hr = bufs.head_rows
        first = rest = None
        for u in units:
            for j in range(n_pairs):
                rows = pair_slice(bufs, u, j)
                res = jnp.dot(bufs.l[rows], m_ref[...], preferred_element_type=F32)
                c = bufs.carry[rows]
                att = jnp.exp(bufs.z[rows] + res[:, :LANES] + c)
                bufs.att[rows] = att.astype(BF16)
                c = c + res[:, LANES:]
                bufs.carry[rows] = c
                for h0 in (0, hr):
                    cf = c[h0:h0 + NARROW_ROWS]
                    first = cf if first is None else jnp.maximum(first, cf)
                    if hr > NARROW_ROWS:
                        cr = c[h0 + NARROW_ROWS:h0 + hr]
                        rest = cr if rest is None else jnp.maximum(rest, cr)
        return jnp.max(first), (None if rest is None else jnp.max(rest))

    def values(bufs, units, step):
        hr = bufs.head_rows
        for u in units:
            k0 = key_start(u, step)
            for j in range(n_pairs):
                cols = slice(j * LANES, (j + 1) * LANES)
                v2 = v_ref[0, pl.ds(k0, CHUNK), cols]
                o2 = jnp.dot(bufs.att[pair_slice(bufs, u, j)], v2, preferred_element_type=F32)
                acc_ref[u * CHUNK:u * CHUNK + hr, cols] += jnp.where(first_head(bufs), o2[:hr], o2[hr:])

    def move_narrow_carries(to_narrow):
        for u in all_units:
            for j in range(n_pairs):
                fb = pair_slice(full, u, j).start
                nb = pair_slice(narrow, u, j).start
                for h in range(2):
                    f_rows = slice(fb + h * CHUNK, fb + h * CHUNK + NARROW_ROWS)
                    n_rows = slice(nb + h * NARROW_ROWS, nb + (h + 1) * NARROW_ROWS)
                    if to_narrow:
                        carry_n[n_rows] = carry_f[f_rows]
                    else:
                        carry_f[f_rows] = carry_n[n_rows]

    def unit0_has_keys(step):
        return qi * n_units - step >= 0

    scores(full, all_units, 0, True)
    c_first, c_rest = weights(full, all_units)

    def full_cond(state):
        step, _, c_rest = state
        return jnp.logical_and(unit0_has_keys(step), c_rest > LOG_WEIGHT_FLOOR)

    def full_body(state):
        step = state[0]
        scores(full, all_units, step, False)
        values(full, all_units, step - 1)
        c_first, c_rest = weights(full, all_units)
        return step + 1, c_first, c_rest

    step, c_first, c_rest = lax.while_loop(full_cond, full_body, (1, c_first, c_rest))
    values(full, all_units, step - 1)

    move_narrow_carries(True)

    def narrow_cond(state):
        step, c_first = state
        return jnp.logical_and(unit0_has_keys(step), c_first > LOG_WEIGHT_FLOOR)

    def narrow_body(state):
        step = state[0]
        scores(narrow, all_units, step, False)
        c_first, _ = weights(narrow, all_units)
        values(narrow, all_units, step)
        return step + 1, c_first

    step, c_first = lax.while_loop(narrow_cond, narrow_body, (step, c_first))

    cm = jnp.maximum(c_first, c_rest)

    @pl.when(jnp.logical_not(unit0_has_keys(step)))
    def _():
        move_narrow_carries(False)

    for u in all_units[1:]:
        def cond(state, u=u):
            step, cm = state
            return jnp.logical_and(qi * n_units + u - step >= 0, cm > LOG_WEIGHT_FLOOR)

        def body(state, u=u):
            step = state[0]
            scores(full, (u,), step, False)
            c_first, c_rest = weights(full, (u,))
            values(full, (u,), step)
            return step + 1, jnp.maximum(c_first, c_rest)

        lax.while_loop(cond, body, (step, cm))
    o_ref[0] = acc_ref[...]


def _attention(q, k, v, mcat):
    b, s, w = q.shape
    n_heads = w // HEAD_DIM
    n_units = ATTN_UNITS
    tq = n_units * CHUNK

    def sweep_scratch(head_rows):
        rows = n_units * n_heads * head_rows
        return [pltpu.VMEM((rows, LANES), BF16),
                pltpu.VMEM((rows, LANES), F32),
                pltpu.VMEM((rows, LANES), BF16),
                pltpu.VMEM((rows, LANES), BF16),
                pltpu.VMEM((rows, LANES), F32)]

    return pl.pallas_call(
        functools.partial(_attn_kernel, n_heads=n_heads, n_units=n_units),
        grid=(b, s // tq),
        in_specs=[pl.BlockSpec((1, tq, w), lambda bi, qi: (bi, qi, 0)),
                  pl.BlockSpec((1, s, w), lambda bi, qi: (bi, 0, 0)),
                  pl.BlockSpec((1, s, w), lambda bi, qi: (bi, 0, 0)),
                  _const_spec(mcat.shape)],
        out_specs=pl.BlockSpec((1, tq, w), lambda bi, qi: (bi, qi, 0)),
        out_shape=jax.ShapeDtypeStruct((b, s, w), F32),
        scratch_shapes=sweep_scratch(CHUNK) + sweep_scratch(NARROW_ROWS) + [pltpu.VMEM((tq, w), F32)],
        compiler_params=pltpu.CompilerParams(dimension_semantics=("arbitrary", "arbitrary"),
                                             vmem_limit_bytes=VMEM_LIMIT),
        name="attention",
    )(q, k, v, mcat)


def _mixer_out_kernel(x_ref, a_ref, yb_ref, p_ref, pprev_ref, sg_ref, sw_ref, sb_ref, pw_ref, ps_ref,
                      mg_ref, wo_ref, gm_ref, o_ref, *, tiles_per_seq):
    i = pl.program_id(0)
    tm = x_ref.shape[0]
    w_a = a_ref.shape[1] // 2
    w_c = p_ref.shape[1]
    gmat = gm_ref[...]
    seq_tile = i % tiles_per_seq

    p = p_ref[...]
    pprev = jnp.where(seq_tile == 0, 0.0, pprev_ref[...])
    acc = jnp.concatenate([pprev, p], axis=0)
    group = lax.broadcasted_iota(jnp.int32, (tm, w_c), 1) // (w_c // len(POOL_WINDOWS))
    pos1 = (seq_tile * tm + lax.broadcasted_iota(jnp.int32, (tm, w_c), 0) + 1).astype(F32)
    win = jnp.zeros((tm, w_c), F32)
    cnt = jnp.ones((tm, w_c), F32)
    span = 1
    for g, wdw in enumerate(POOL_WINDOWS):
        while span < wdw:
            acc = acc + pltpu.roll(acc, span, 0)
            span *= 2
        win = jnp.where(group == g, acc[HALO:], win)
        cnt = jnp.where(group == g, jnp.minimum(pos1, float(wdw)), cnt)
    d = win / cnt - p
    yc = jnp.dot(d.astype(BF16), pw_ref[...], preferred_element_type=F32) * ps_ref[...]

    n_h = w_a // HEAD_DIM
    r = lax.broadcasted_iota(jnp.int32, (CHUNK, CHUNK), 0)
    cidx = lax.broadcasted_iota(jnp.int32, (CHUNK, CHUNK), 1)
    wm_all = jnp.concatenate([jnp.where(cidx <= r, sw_ref[h], 0.0) for h in range(n_h)], axis=1).astype(BF16)
    head_of_lane = lax.broadcasted_iota(jnp.int32, (CHUNK, w_a), 1) // HEAD_DIM

    a = a_ref[...]
    ga = 0.5 * a * (1.0 + lax.erf(a * (2.0 ** -0.5)))
    u = ga[:, :w_a]
    vn = _head_rmsnorm(ga[:, w_a:], sg_ref[...], gmat).astype(BF16)
    ya_parts = []
    for c in range(tm // CHUNK):
        vb = vn[c * CHUNK:(c + 1) * CHUNK]
        v_bd = jnp.concatenate([jnp.where(head_of_lane == h, vb, jnp.zeros_like(vb)) for h in range(n_h)], axis=0)
        s = sb_ref[...] + jnp.dot(wm_all, v_bd, preferred_element_type=F32)
        ya_parts.append(u[c * CHUNK:(c + 1) * CHUNK] * s)
    ya = jnp.concatenate(ya_parts, axis=0)

    y = jnp.concatenate([ya, yb_ref[...], yc], axis=1)
    yn = _head_rmsnorm(y, mg_ref[...], gmat).astype(BF16)
    o_ref[...] = x_ref[...] + jnp.dot(yn, wo_ref[...], preferred_element_type=F32)


def _mixer_out(x, a, yb, p, sg, sw, sb_full, pw_bd, ps, mg, wo, gmat, seq_len):
    n, d = x.shape
    tm = ROW_TILE
    hb = tm // HALO
    row = lambda c: pl.BlockSpec((tm, c), lambda i: (i, 0))
    prev = pl.BlockSpec((HALO, p.shape[1]), lambda i: (jnp.maximum(i * hb - 1, 0), 0))
    return pl.pallas_call(
        functools.partial(_mixer_out_kernel, tiles_per_seq=seq_len // tm),
        grid=(n // tm,),
        in_specs=[row(d), row(a.shape[1]), row(yb.shape[1]), row(p.shape[1]), prev,
                  _const_spec(sg.shape), _const_spec(sw.shape), _const_spec(sb_full.shape),
                  _const_spec(pw_bd.shape), _const_spec(ps.shape), _const_spec(mg.shape),
                  _const_spec(wo.shape), _const_spec(gmat.shape)],
        out_specs=row(d),
        out_shape=jax.ShapeDtypeStruct((n, d), F32),
        compiler_params=pltpu.CompilerParams(dimension_semantics=("arbitrary",), vmem_limit_bytes=VMEM_LIMIT),
        name="mixer_out",
    )(x, a, yb, p, p, sg, sw, sb_full, pw_bd, ps, mg, wo, gmat)


def _ffn_kernel(x_ref, g_ref, wup_ref, cw_ref, cb_ref, wdn_ref, fg_ref, o_ref, h_ref, z_ref, ztail_ref, act_ref,
                *, tiles_per_seq, final_norm):
    i = pl.program_id(0)
    tm = x_ref.shape[0]
    d_ff = wdn_ref.shape[0]
    fc = FFN_CHUNK
    n_fc = d_ff // fc
    h_ref[...] = _row_rmsnorm(x_ref[...], g_ref[...]).astype(BF16)

    @pl.when(i % tiles_per_seq == 0)
    def _():
        ztail_ref[...] = jnp.zeros_like(ztail_ref)

    def gate_up(ref, c):
        return jnp.concatenate([ref[:, c * fc:(c + 1) * fc], ref[:, d_ff + c * fc:d_ff + (c + 1) * fc]], axis=1)

    def up_proj(c, buf):
        z = jnp.dot(h_ref[...], gate_up(wup_ref, c), preferred_element_type=F32)
        z_ref[buf, HALO - SUBLANES:HALO] = ztail_ref[c]
        z_ref[buf, HALO:] = z
        ztail_ref[c] = z[tm - SUBLANES:]

    def conv_gate(c, buf):
        zb = z_ref.at[buf]
        cw = gate_up(cw_ref, c)
        zc = (gate_up(cb_ref, c) + zb[HALO - 2:HALO - 2 + tm] * cw[0:1] + zb[HALO - 1:HALO - 1 + tm] * cw[1:2]
              + zb[HALO:] * cw[2:3])
        gate = zc[:, :fc]
        act_ref[:, c * fc:(c + 1) * fc] = (gate * jax.nn.sigmoid(gate) * zc[:, fc:]).astype(BF16)

    up_proj(0, 0)
    for c in range(n_fc):
        if c + 1 < n_fc:
            up_proj(c + 1, (c + 1) % 2)
        conv_gate(c, c % 2)

    out = x_ref[...]
    for k0 in range(0, d_ff, MXU_N):
        out = out + jnp.dot(act_ref[:, k0:k0 + MXU_N], wdn_ref[k0:k0 + MXU_N], preferred_element_type=F32)
    if final_norm:
        out = _row_rmsnorm(out, fg_ref[...])
    o_ref[...] = out


def _ffn(x, g, wup, cw, cb, wdn, fg, seq_len, final_norm):
    n, d = x.shape
    tm = ROW_TILE
    d_ff = wdn.shape[0]
    row = pl.BlockSpec((tm, d), lambda i: (i, 0))
    return pl.pallas_call(
        functools.partial(_ffn_kernel, tiles_per_seq=seq_len // tm, final_norm=final_norm),
        grid=(n // tm,),
        in_specs=[row, _const_spec(g.shape), _const_spec(wup.shape), _const_spec(cw.shape),
                  _const_spec(cb.shape), _const_spec(wdn.shape), _const_spec(fg.shape)],
        out_specs=row,
        out_shape=jax.ShapeDtypeStruct((n, d), F32),
        scratch_shapes=[pltpu.VMEM((tm, d), BF16),
                        pltpu.VMEM((2, HALO + tm, 2 * FFN_CHUNK), F32),
                        pltpu.VMEM((d_ff // FFN_CHUNK, SUBLANES, 2 * FFN_CHUNK), F32),
                        pltpu.VMEM((tm, d_ff), BF16)],
        compiler_params=pltpu.CompilerParams(dimension_semantics=("arbitrary",), vmem_limit_bytes=VMEM_LIMIT),
        name="ffn",
    )(x, g, wup, cw, cb, wdn, fg)


def _constants():
    j = jnp.arange(CHUNK)
    tail = (j[:, None] >= j[None, :]).astype(BF16)
    mcat = -jnp.concatenate([tail, jnp.ones((CHUNK, LANES), BF16)], axis=1)
    i = jnp.arange(MXU_N)
    gmat = jnp.where(i[:, None] // HEAD_DIM == i[None, :] // HEAD_DIM, 1.0 / HEAD_DIM, 0.0).astype(BF16)
    return mcat, gmat


def kernel(x, norm1_g, w_in, sgu_norm_g, sgu_w, sgu_b, pool_w, pool_scale, mix_norm_g, w_o, norm2_g, w_up, conv_w,
           conv_b, w_down, final_g):
    b, s, d = x.shape
    depth = w_in.shape[0]
    w_a = sgu_norm_g.shape[1]
    w_c = pool_scale.shape[1]
    w_b = (w_in.shape[2] - 2 * w_a - w_c) // 3
    assert w_down.shape[1] % FFN_CHUNK == 0 and s % ROW_TILE == 0 and w_a + w_b + w_c == d
    mcat, gmat = _constants()

    xf = x.reshape(b * s, d)
    for l in range(depth):
        a, q, k, v, p = _in_proj(xf, norm1_g[l][None], w_in[l].astype(BF16), 2 * w_a, w_b, w_c)
        yb = _attention(q.reshape(b, s, w_b), k.reshape(b, s, w_b), v.reshape(b, s, w_b), mcat)
        sb_full = jnp.repeat(sgu_b[l].T, HEAD_DIM, axis=1)
        pw_bd = jax.scipy.linalg.block_diag(*[pool_w[l, g] for g in range(pool_w.shape[1])]).astype(BF16)
        x1 = _mixer_out(xf, a, yb.reshape(b * s, w_b), p, sgu_norm_g[l][None], sgu_w[l], sb_full, pw_bd,
                        pool_scale[l][None], mix_norm_g[l][None], w_o[l].astype(BF16), gmat, s)
        xf = _ffn(x1, norm2_g[l][None], w_up[l].astype(BF16), conv_w[l], conv_b[l][None], w_down[l].astype(BF16),
                  final_g[None], s, l == depth - 1)
    return xf.reshape(b, s, d)
```

```python
import functools
from typing import Any, NamedTuple

import jax
import jax.numpy as jnp
from jax import lax
from jax.experimental import pallas as pl
from jax.experimental.pallas import tpu as pltpu

F32 = jnp.float32
BF16 = jnp.bfloat16

EPS = 1e-6
HEAD_DIM = 64
CHUNK = 128
POOL_WINDOWS = (2, 4, 8, 16)
HALO = 16
LANES = 128
SUBLANES = 8
MXU_N = 256
FFN_CHUNK = MXU_N
ROW_TILE = 1024
FFN_ROW_TILE = 512
ATTN_UNITS = 4
NARROW_ROWS = 64
VMEM_LIMIT = 56 * 1024 * 1024
LOG_WEIGHT_FLOOR = -104.0
MASKED_SCORE = -1e30


def _row_rmsnorm(x, g):
    return x * lax.rsqrt(jnp.mean(x * x, axis=-1, keepdims=True) + EPS) * g


def _head_rmsnorm(y, g, gmat):
    y2 = (y * y).astype(BF16)
    parts = []
    for c in range(y.shape[1] // MXU_N):
        parts.append(jnp.dot(y2[:, c * MXU_N:(c + 1) * MXU_N], gmat, preferred_element_type=F32))
    ms = parts[0] if len(parts) == 1 else jnp.concatenate(parts, axis=1)
    return y * lax.rsqrt(ms + EPS) * g


def _const_spec(shape):
    return pl.BlockSpec(shape, lambda *_: (0,) * len(shape))


def _in_proj_kernel(x_ref, g_ref, w_ref, a_ref, q_ref, k_ref, v_ref, p_ref, *, w_a2, w_b):
    h = _row_rmsnorm(x_ref[...], g_ref[...]).astype(BF16)

    def proj(c0, c1):
        return jnp.dot(h, w_ref[:, c0:c1], preferred_element_type=F32)

    a_ref[...] = proj(0, w_a2)
    q_ref[...] = (proj(w_a2, w_a2 + w_b) * (HEAD_DIM ** -0.5)).astype(BF16)
    k_ref[...] = proj(w_a2 + w_b, w_a2 + 2 * w_b).astype(BF16)
    v_ref[...] = proj(w_a2 + 2 * w_b, w_a2 + 3 * w_b).astype(BF16)
    p_ref[...] = proj(w_a2 + 3 * w_b, w_ref.shape[1])


def _in_proj(x, g, w, w_a2, w_b, w_c):
    n, d = x.shape
    tm = ROW_TILE
    row = lambda c: pl.BlockSpec((tm, c), lambda i: (i, 0))
    return pl.pallas_call(
        functools.partial(_in_proj_kernel, w_a2=w_a2, w_b=w_b),
        grid=(n // tm,),
        in_specs=[row(d), _const_spec((1, d)), _const_spec(w.shape)],
        out_specs=[row(w_a2), row(w_b), row(w_b), row(w_b), row(w_c)],
        out_shape=[jax.ShapeDtypeStruct((n, w_a2), F32),
                   jax.ShapeDtypeStruct((n, w_b), BF16),
                   jax.ShapeDtypeStruct((n, w_b), BF16),
                   jax.ShapeDtypeStruct((n, w_b), BF16),
                   jax.ShapeDtypeStruct((n, w_c), F32)],
        compiler_params=pltpu.CompilerParams(dimension_semantics=("arbitrary",), vmem_limit_bytes=VMEM_LIMIT),
        name="in_proj",
    )(x, g, w)


class _SweepBuffers(NamedTuple):
    qs: Any
    z: Any
    l: Any
    att: Any
    carry: Any
    head_rows: int


def _attn_kernel(q_ref, k_ref, v_ref, m_ref, o_ref, qs_f, z_f, l_f, att_f, carry_f, qs_n, z_n, l_n, att_n, carry_n,
                 acc_ref, *, n_heads, n_units):
    qi = pl.program_id(1)
    n_pairs = n_heads // 2
    full = _SweepBuffers(qs_f, z_f, l_f, att_f, carry_f, CHUNK)
    narrow = _SweepBuffers(qs_n, z_n, l_n, att_n, carry_n, NARROW_ROWS)
    nt_dims = (((1,), (1,)), ((), ()))
    sign_bit = jnp.uint32(0x80000000)
    all_units = tuple(range(n_units))
    lane = lax.broadcasted_iota(jnp.int32, (2 * CHUNK, LANES), 1)
    row = lax.broadcasted_iota(jnp.int32, (2 * CHUNK, LANES), 0)
    non_causal_bias = jnp.where(lane < (row % CHUNK), 0.0, MASKED_SCORE)

    def pair_slice(bufs, u, j):
        base = (u * n_pairs + j) * 2 * bufs.head_rows
        return slice(base, base + 2 * bufs.head_rows)

    def first_head(bufs):
        return lax.broadcasted_iota(jnp.int32, (bufs.head_rows, LANES), 1) < HEAD_DIM

    def key_start(u, step):
        return pl.multiple_of((qi * n_units + u - step) * CHUNK, CHUNK)

    for bufs in (full, narrow):
        hr = bufs.head_rows
        ln = lax.broadcasted_iota(jnp.int32, (2 * hr, LANES), 1)
        rw = lax.broadcasted_iota(jnp.int32, (2 * hr, LANES), 0)
        own_lanes = (ln < HEAD_DIM) == (rw < hr)
        for u in all_units:
            for j in range(n_pairs):
                q2 = q_ref[0, u * CHUNK:u * CHUNK + hr, j * LANES:(j + 1) * LANES]
                q4 = jnp.concatenate([q2, q2], axis=0)
                bufs.qs[pair_slice(bufs, u, j)] = jnp.where(own_lanes, q4, jnp.zeros_like(q4))
    acc_ref[...] = jnp.zeros_like(acc_ref)
    carry_f[...] = jnp.zeros_like(carry_f)

    def scores(bufs, units, step, diagonal):
        for u in units:
            k0 = key_start(u, step)
            for j in range(n_pairs):
                rows = pair_slice(bufs, u, j)
                k2 = k_ref[0, pl.ds(k0, CHUNK), j * LANES:(j + 1) * LANES]
                z = lax.dot_general(bufs.qs[rows], k2, nt_dims, preferred_element_type=F32)
                if diagonal:
                    z = z + non_causal_bias
                neg_abs = lax.bitcast_convert_type(lax.bitcast_convert_type(z, jnp.uint32) | sign_bit, F32)
                softplus = jnp.maximum(z, 0.0) + jnp.log(1.0 + jnp.exp(neg_abs))
                bufs.z[rows] = z
                bufs.l[rows] = softplus.astype(BF16)

    def weights(bufs, units):
        hr = bufs.head_rows
        first = rest = None
        for u in units:
            for j in range(n_pairs):
                rows = pair_slice(bufs, u, j)
                res = jnp.dot(bufs.l[rows], m_ref[...], preferred_element_type=F32)
                c = bufs.carry[rows]
                att = jnp.exp(bufs.z[rows] + res[:, :LANES] + c)
                bufs.att[rows] = att.astype(BF16)
                c = c + res[:, LANES:]
                bufs.carry[rows] = c
                for h0 in (0, hr):
                    cf = c[h0:h0 + NARROW_ROWS]
                    first = cf if first is None else jnp.maximum(first, cf)
                    if hr > NARROW_ROWS:
                        cr = c[h0 + NARROW_ROWS:h0 + hr]
                        rest = cr if rest is None else jnp.maximum(rest, cr)
        return jnp.max(first), (None if rest is None else jnp.max(rest))

    def values(bufs, units, step):
        hr = bufs.head_rows
        for u in units:
            k0 = key_start(u, step)
            for j in range(n_pairs):
                cols = slice(j * LANES, (j + 1) * LANES)
                v2 = v_ref[0, pl.ds(k0, CHUNK), cols]
                o2 = jnp.dot(bufs.att[pair_slice(bufs, u, j)], v2, preferred_element_type=F32)
                acc_ref[u * CHUNK:u * CHUNK + hr, cols] += jnp.where(first_head(bufs), o2[:hr], o2[hr:])

    def move_narrow_carries(to_narrow):
        for u in all_units:
            for j in range(n_pairs):
                fb = pair_slice(full, u, j).start
                nb = pair_slice(narrow, u, j).start
                for h in range(2):
                    f_rows = slice(fb + h * CHUNK, fb + h * CHUNK + NARROW_ROWS)
                    n_rows = slice(nb + h * NARROW_ROWS, nb + (h + 1) * NARROW_ROWS)
                    if to_narrow:
                        carry_n[n_rows] = carry_f[f_rows]
                    else:
                        carry_f[f_rows] = carry_n[n_rows]

    def unit0_has_keys(step):
        return qi * n_units - step >= 0

    scores(full, all_units, 0, True)
    c_first, c_rest = weights(full, all_units)

    def full_cond(state):
        step, _, c_rest = state
        return jnp.logical_and(unit0_has_keys(step), c_rest > LOG_WEIGHT_FLOOR)

    def full_body(state):
        step = state[0]
        scores(full, all_units, step, False)
        values(full, all_units, step - 1)
        c_first, c_rest = weights(full, all_units)
        return step + 1, c_first, c_rest

    step, c_first, c_rest = lax.while_loop(full_cond, full_body, (1, c_first, c_rest))
    values(full, all_units, step - 1)

    move_narrow_carries(True)

    def narrow_cond(state):
        step, c_first = state
        return jnp.logical_and(unit0_has_keys(step), c_first > LOG_WEIGHT_FLOOR)

    def narrow_body(state):
        step = state[0]
        scores(narrow, all_units, step, False)
        c_first, _ = weights(narrow, all_units)
        values(narrow, all_units, step)
        return step + 1, c_first

    step, c_first = lax.while_loop(narrow_cond, narrow_body, (step, c_first))

    cm = jnp.maximum(c_first, c_rest)

    @pl.when(jnp.logical_not(unit0_has_keys(step)))
    def _():
        move_narrow_carries(False)

    for u in all_units[1:]:
        def cond(state, u=u):
            step, cm = state
            return jnp.logical_and(qi * n_units + u - step >= 0, cm > LOG_WEIGHT_FLOOR)

        def body(state, u=u):
            step = state[0]
            scores(full, (u,), step, False)
            c_first, c_rest = weights(full, (u,))
            values(full, (u,), step)
            return step + 1, jnp.maximum(c_first, c_rest)

        lax.while_loop(cond, body, (step, cm))
    o_ref[0] = acc_ref[...]


def _attention(q, k, v, mcat):
    b, s, w = q.shape
    n_heads = w // HEAD_DIM
    n_units = ATTN_UNITS
    tq = n_units * CHUNK

    def sweep_scratch(head_rows):
        rows = n_units * n_heads * head_rows
        return [pltpu.VMEM((rows, LANES), BF16),
                pltpu.VMEM((rows, LANES), F32),
                pltpu.VMEM((rows, LANES), BF16),
                pltpu.VMEM((rows, LANES), BF16),
                pltpu.VMEM((rows, LANES), F32)]

    return pl.pallas_call(
        functools.partial(_attn_kernel, n_heads=n_heads, n_units=n_units),
        grid=(b, s // tq),
        in_specs=[pl.BlockSpec((1, tq, w), lambda bi, qi: (bi, qi, 0)),
                  pl.BlockSpec((1, s, w), lambda bi, qi: (bi, 0, 0)),
                  pl.BlockSpec((1, s, w), lambda bi, qi: (bi, 0, 0)),
                  _const_spec(mcat.shape)],
        out_specs=pl.BlockSpec((1, tq, w), lambda bi, qi: (bi, qi, 0)),
        out_shape=jax.ShapeDtypeStruct((b, s, w), F32),
        scratch_shapes=sweep_scratch(CHUNK) + sweep_scratch(NARROW_ROWS) + [pltpu.VMEM((tq, w), F32)],
        compiler_params=pltpu.CompilerParams(dimension_semantics=("arbitrary", "arbitrary"),
                                             vmem_limit_bytes=VMEM_LIMIT),
        name="attention",
    )(q, k, v, mcat)


def _mixer_out_kernel(x_ref, a_ref, yb_ref, p_ref, pprev_ref, sg_ref, sw_ref, sb_ref, pw_ref, ps_ref,
                      mg_ref, wo_ref, gm_ref, o_ref, *, tiles_per_seq):
    i = pl.program_id(0)
    tm = x_ref.shape[0]
    w_a = a_ref.shape[1] // 2
    w_c = p_ref.shape[1]
    gmat = gm_ref[...]
    seq_tile = i % tiles_per_seq

    p = p_ref[...]
    pprev = jnp.where(seq_tile == 0, 0.0, pprev_ref[...])
    acc = jnp.concatenate([pprev, p], axis=0)
    group = lax.broadcasted_iota(jnp.int32, (tm, w_c), 1) // (w_c // len(POOL_WINDOWS))
    pos1 = (seq_tile * tm + lax.broadcasted_iota(jnp.int32, (tm, w_c), 0) + 1).astype(F32)
    win = jnp.zeros((tm, w_c), F32)
    cnt = jnp.ones((tm, w_c), F32)
    span = 1
    for g, wdw in enumerate(POOL_WINDOWS):
        while span < wdw:
            acc = acc + pltpu.roll(acc, span, 0)
            span *= 2
        win = jnp.where(group == g, acc[HALO:], win)
        cnt = jnp.where(group == g, jnp.minimum(pos1, float(wdw)), cnt)
    d = win / cnt - p
    yc = jnp.dot(d.astype(BF16), pw_ref[...], preferred_element_type=F32) * ps_ref[...]

    n_h = w_a // HEAD_DIM
    r = lax.broadcasted_iota(jnp.int32, (CHUNK, CHUNK), 0)
    cidx = lax.broadcasted_iota(jnp.int32, (CHUNK, CHUNK), 1)
    wm_all = jnp.concatenate([jnp.where(cidx <= r, sw_ref[h], 0.0) for h in range(n_h)], axis=1).astype(BF16)
    head_of_lane = lax.broadcasted_iota(jnp.int32, (CHUNK, w_a), 1) // HEAD_DIM

    a = a_ref[...]
    ga = 0.5 * a * (1.0 + lax.erf(a * (2.0 ** -0.5)))
    u = ga[:, :w_a]
    vn = _head_rmsnorm(ga[:, w_a:], sg_ref[...], gmat).astype(BF16)
    ya_parts = []
    for c in range(tm // CHUNK):
        vb = vn[c * CHUNK:(c + 1) * CHUNK]
        v_bd = jnp.concatenate([jnp.where(head_of_lane == h, vb, jnp.zeros_like(vb)) for h in range(n_h)], axis=0)
        s = sb_ref[...] + jnp.dot(wm_all, v_bd, preferred_element_type=F32)
        ya_parts.append(u[c * CHUNK:(c + 1) * CHUNK] * s)
    ya = jnp.concatenate(ya_parts, axis=0)

    y = jnp.concatenate([ya, yb_ref[...], yc], axis=1)
    yn = _head_rmsnorm(y, mg_ref[...], gmat).astype(BF16)
    o_ref[...] = x_ref[...] + jnp.dot(yn, wo_ref[...], preferred_element_type=F32)


def _mixer_out(x, a, yb, p, sg, sw, sb_full, pw_bd, ps, mg, wo, gmat, seq_len):
    n, d = x.shape
    tm = ROW_TILE
    hb = tm // HALO
    row = lambda c: pl.BlockSpec((tm, c), lambda i: (i, 0))
    prev = pl.BlockSpec((HALO, p.shape[1]), lambda i: (jnp.maximum(i * hb - 1, 0), 0))
    return pl.pallas_call(
        functools.partial(_mixer_out_kernel, tiles_per_seq=seq_len // tm),
        grid=(n // tm,),
        in_specs=[row(d), row(a.shape[1]), row(yb.shape[1]), row(p.shape[1]), prev,
                  _const_spec(sg.shape), _const_spec(sw.shape), _const_spec(sb_full.shape),
                  _const_spec(pw_bd.shape), _const_spec(ps.shape), _const_spec(mg.shape),
                  _const_spec(wo.shape), _const_spec(gmat.shape)],
        out_specs=row(d),
        out_shape=jax.ShapeDtypeStruct((n, d), F32),
        compiler_params=pltpu.CompilerParams(dimension_semantics=("arbitrary",), vmem_limit_bytes=VMEM_LIMIT),
        name="mixer_out",
    )(x, a, yb, p, p, sg, sw, sb_full, pw_bd, ps, mg, wo, gmat)


def _ffn_kernel(x_ref, g_ref, wup_ref, cw_ref, cb_ref, wdn_ref, fg_ref, o_ref, h_ref, z_ref, ztail_ref, act_ref,
                *, tiles_per_seq, final_norm):
    i = pl.program_id(0)
    tm = x_ref.shape[0]
    d_ff = wdn_ref.shape[0]
    fc = FFN_CHUNK
    n_fc = d_ff // fc
    h_ref[...] = _row_rmsnorm(x_ref[...], g_ref[...]).astype(BF16)

    @pl.when(i % tiles_per_seq == 0)
    def _():
        ztail_ref[...] = jnp.zeros_like(ztail_ref)

    def gate_up(ref, c):
        return jnp.concatenate([ref[:, c * fc:(c + 1) * fc], ref[:, d_ff + c * fc:d_ff + (c + 1) * fc]], axis=1)

    def up_proj(c, buf):
        z = jnp.dot(h_ref[...], gate_up(wup_ref, c), preferred_element_type=F32)
        z_ref[buf, HALO - SUBLANES:HALO] = ztail_ref[c]
        z_ref[buf, HALO:] = z
        ztail_ref[c] = z[tm - SUBLANES:]

    def conv_gate(c, buf):
        zb = z_ref.at[buf]
        cw = gate_up(cw_ref, c)
        zc = (gate_up(cb_ref, c) + zb[HALO - 2:HALO - 2 + tm] * cw[0:1] + zb[HALO - 1:HALO - 1 + tm] * cw[1:2]
              + zb[HALO:] * cw[2:3])
        gate = zc[:, :fc]
        act_ref[:, c * fc:(c + 1) * fc] = (gate * jax.nn.sigmoid(gate) * zc[:, fc:]).astype(BF16)

    up_proj(0, 0)
    for c in range(n_fc):
        if c + 1 < n_fc:
            up_proj(c + 1, (c + 1) % 2)
        conv_gate(c, c % 2)

    out = x_ref[...]
    for k0 in range(0, d_ff, MXU_N):
        out = out + jnp.dot(act_ref[:, k0:k0 + MXU_N], wdn_ref[k0:k0 + MXU_N], preferred_element_type=F32)
    if final_norm:
        out = _row_rmsnorm(out, fg_ref[...])
    o_ref[...] = out


def _ffn(x, g, wup, cw, cb, wdn, fg, seq_len, final_norm):
    n, d = x.shape
    tm = FFN_ROW_TILE
    d_ff = wdn.shape[0]
    row = pl.BlockSpec((tm, d), lambda i: (i, 0))
    return pl.pallas_call(
        functools.partial(_ffn_kernel, tiles_per_seq=seq_len // tm, final_norm=final_norm),
        grid=(n // tm,),
        in_specs=[row, _const_spec(g.shape), _const_spec(wup.shape), _const_spec(cw.shape),
                  _const_spec(cb.shape), _const_spec(wdn.shape), _const_spec(fg.shape)],
        out_specs=row,
        out_shape=jax.ShapeDtypeStruct((n, d), F32),
        scratch_shapes=[pltpu.VMEM((tm, d), BF16),
                        pltpu.VMEM((2, HALO + tm, 2 * FFN_CHUNK), F32),
                        pltpu.VMEM((d_ff // FFN_CHUNK, SUBLANES, 2 * FFN_CHUNK), F32),
                        pltpu.VMEM((tm, d_ff), BF16)],
        compiler_params=pltpu.CompilerParams(dimension_semantics=("arbitrary",), vmem_limit_bytes=VMEM_LIMIT),
        name="ffn",
    )(x, g, wup, cw, cb, wdn, fg)


def _constants():
    j = jnp.arange(CHUNK)
    tail = (j[:, None] >= j[None, :]).astype(BF16)
    mcat = -jnp.concatenate([tail, jnp.ones((CHUNK, LANES), BF16)], axis=1)
    i = jnp.arange(MXU_N)
    gmat = jnp.where(i[:, None] // HEAD_DIM == i[None, :] // HEAD_DIM, 1.0 / HEAD_DIM, 0.0).astype(BF16)
    return mcat, gmat


def kernel(x, norm1_g, w_in, sgu_norm_g, sgu_w, sgu_b, pool_w, pool_scale, mix_norm_g, w_o, norm2_g, w_up, conv_w,
           conv_b, w_down, final_g):
    b, s, d = x.shape
    depth = w_in.shape[0]
    w_a = sgu_norm_g.shape[1]
    w_c = pool_scale.shape[1]
    w_b = (w_in.shape[2] - 2 * w_a - w_c) // 3
    assert w_down.shape[1] % FFN_CHUNK == 0 and s % ROW_TILE == 0 and w_a + w_b + w_c == d
    mcat, gmat = _constants()

    xf = x.reshape(b * s, d)
    for l in range(depth):
        a, q, k, v, p = _in_proj(xf, norm1_g[l][None], w_in[l].astype(BF16), 2 * w_a, w_b, w_c)
        yb = _attention(q.reshape(b, s, w_b), k.reshape(b, s, w_b), v.reshape(b, s, w_b), mcat)
        sb_full = jnp.repeat(sgu_b[l].T, HEAD_DIM, axis=1)
        pw_bd = jax.scipy.linalg.block_diag(*[pool_w[l, g] for g in range(pool_w.shape[1])]).astype(BF16)
        x1 = _mixer_out(xf, a, yb.reshape(b * s, w_b), p, sgu_norm_g[l][None], sgu_w[l], sb_full, pw_bd,
                        pool_scale[l][None], mix_norm_g[l][None], w_o[l].astype(BF16), gmat, s)
        xf = _ffn(x1, norm2_g[l][None], w_up[l].astype(BF16), conv_w[l], conv_b[l][None], w_down[l].astype(BF16),
                  final_g[None], s, l == depth - 1)
    return xf.reshape(b, s, d)
```

```python
import functools
from typing import Any, NamedTuple

import jax
import jax.numpy as jnp
from jax import lax
from jax.experimental import pallas as pl
from jax.experimental.pallas import tpu as pltpu

F32 = jnp.float32
BF16 = jnp.bfloat16

EPS = 1e-6
HEAD_DIM = 64
CHUNK = 128
POOL_WINDOWS = (2, 4, 8, 16)
HALO = 16
LANES = 128
SUBLANES = 8
MXU_N = 256
FFN_CHUNK = MXU_N
ROW_TILE = 1024
FFN_ROW_TILE = 512
ATTN_UNITS = 4
NARROW_ROWS = 64
VMEM_LIMIT = 56 * 1024 * 1024
LOG_WEIGHT_FLOOR = -104.0
MASKED_SCORE = -1e30


def _row_rmsnorm(x, g):
    return x * lax.rsqrt(jnp.mean(x * x, axis=-1, keepdims=True) + EPS) * g


def _head_rmsnorm(y, g, gmat):
    y2 = (y * y).astype(BF16)
    parts = []
    for c in range(y.shape[1] // MXU_N):
        parts.append(jnp.dot(y2[:, c * MXU_N:(c + 1) * MXU_N], gmat, preferred_element_type=F32))
    ms = parts[0] if len(parts) == 1 else jnp.concatenate(parts, axis=1)
    return y * lax.rsqrt(ms + EPS) * g


def _const_spec(shape):
    return pl.BlockSpec(shape, lambda *_: (0,) * len(shape))


def _layer_spec(stacked, layer):
    return pl.BlockSpec((None,) + stacked.shape[1:], lambda *_: (layer,) + (0,) * (stacked.ndim - 1))


def _in_proj_kernel(x_ref, g_ref, w_ref, a_ref, q_ref, k_ref, v_ref, p_ref, *, w_a2, w_b):
    h = _row_rmsnorm(x_ref[...], g_ref[...]).astype(BF16)

    def proj(c0, c1):
        return jnp.dot(h, w_ref[:, c0:c1], preferred_element_type=F32)

    a_ref[...] = proj(0, w_a2)
    q_ref[...] = (proj(w_a2, w_a2 + w_b) * (HEAD_DIM ** -0.5)).astype(BF16)
    k_ref[...] = proj(w_a2 + w_b, w_a2 + 2 * w_b).astype(BF16)
    v_ref[...] = proj(w_a2 + 2 * w_b, w_a2 + 3 * w_b).astype(BF16)
    p_ref[...] = proj(w_a2 + 3 * w_b, w_ref.shape[1])


def _in_proj(x, g, w_all, layer, w_a2, w_b, w_c):
    n, d = x.shape
    tm = ROW_TILE
    row = lambda c: pl.BlockSpec((tm, c), lambda i: (i, 0))
    return pl.pallas_call(
        functools.partial(_in_proj_kernel, w_a2=w_a2, w_b=w_b),
        grid=(n // tm,),
        in_specs=[row(d), _const_spec((1, d)), _layer_spec(w_all, layer)],
        out_specs=[row(w_a2), row(w_b), row(w_b), row(w_b), row(w_c)],
        out_shape=[jax.ShapeDtypeStruct((n, w_a2), F32),
                   jax.ShapeDtypeStruct((n, w_b), BF16),
                   jax.ShapeDtypeStruct((n, w_b), BF16),
                   jax.ShapeDtypeStruct((n, w_b), BF16),
                   jax.ShapeDtypeStruct((n, w_c), F32)],
        compiler_params=pltpu.CompilerParams(dimension_semantics=("arbitrary",), vmem_limit_bytes=VMEM_LIMIT),
        name="in_proj",
    )(x, g, w_all)


class _SweepBuffers(NamedTuple):
    qs: Any
    z: Any
    l: Any
    att: Any
    carry: Any
    head_rows: int


def _attn_kernel(q_ref, k_ref, v_ref, m_ref, o_ref, qs_f, z_f, l_f, att_f, carry_f, qs_n, z_n, l_n, att_n, carry_n,
                 acc_ref, *, n_heads, n_units):
    qi = pl.program_id(1)
    n_pairs = n_heads // 2
    full = _SweepBuffers(qs_f, z_f, l_f, att_f, carry_f, CHUNK)
    narrow = _SweepBuffers(qs_n, z_n, l_n, att_n, carry_n, NARROW_ROWS)
    nt_dims = (((1,), (1,)), ((), ()))
    sign_bit = jnp.uint32(0x80000000)
    all_units = tuple(range(n_units))
    lane = lax.broadcasted_iota(jnp.int32, (2 * CHUNK, LANES), 1)
    row = lax.broadcasted_iota(jnp.int32, (2 * CHUNK, LANES), 0)
    non_causal_bias = jnp.where(lane < (row % CHUNK), 0.0, MASKED_SCORE)

    def pair_slice(bufs, u, j):
        base = (u * n_pairs + j) * 2 * bufs.head_rows
        return slice(base, base + 2 * bufs.head_rows)

    def first_head(bufs):
        return lax.broadcasted_iota(jnp.int32, (bufs.head_rows, LANES), 1) < HEAD_DIM

    def key_start(u, step):
        return pl.multiple_of((qi * n_units + u - step) * CHUNK, CHUNK)

    for bufs in (full, narrow):
        hr = bufs.head_rows
        ln = lax.broadcasted_iota(jnp.int32, (2 * hr, LANES), 1)
        rw = lax.broadcasted_iota(jnp.int32, (2 * hr, LANES), 0)
        own_lanes = (ln < HEAD_DIM) == (rw < hr)
        for u in all_units:
            for j in range(n_pairs):
                q2 = q_ref[0, u * CHUNK:u * CHUNK + hr, j * LANES:(j + 1) * LANES]
                q4 = jnp.concatenate([q2, q2], axis=0)
                bufs.qs[pair_slice(bufs, u, j)] = jnp.where(own_lanes, q4, jnp.zeros_like(q4))
    acc_ref[...] = jnp.zeros_like(acc_ref)

    def scores(bufs, units, step, diagonal):
        for u in units:
            k0 = key_start(u, step)
            for j in range(n_pairs):
                rows = pair_slice(bufs, u, j)
                k2 = k_ref[0, pl.ds(k0, CHUNK), j * LANES:(j + 1) * LANES]
                z = lax.dot_general(bufs.qs[rows], k2, nt_dims, preferred_element_type=F32)
                if diagonal:
                    z = z + non_causal_bias
                neg_abs = lax.bitcast_convert_type(lax.bitcast_convert_type(z, jnp.uint32) | sign_bit, F32)
                softplus = jnp.maximum(z, 0.0) + jnp.log(1.0 + jnp.exp(neg_abs))
                bufs.z[rows] = z
                bufs.l[rows] = softplus.astype(BF16)

    def weights(bufs, units, first_step=False):
        hr = bufs.head_rows
        first = rest = None
        for u in units:
            for j in range(n_pairs):
                rows = pair_slice(bufs, u, j)
                res = jnp.dot(bufs.l[rows], m_ref[...], preferred_element_type=F32)
                log_att = bufs.z[rows] + res[:, :LANES]
                c = res[:, LANES:]
                if not first_step:
                    c_in = bufs.carry[rows]
                    log_att = log_att + c_in
                    c = c + c_in
                bufs.att[rows] = jnp.exp(log_att).astype(BF16)
                bufs.carry[rows] = c
                for h0 in (0, hr):
                    cf = c[h0:h0 + NARROW_ROWS]
                    first = cf if first is None else jnp.maximum(first, cf)
                    if hr > NARROW_ROWS:
                        cr = c[h0 + NARROW_ROWS:h0 + hr]
                        rest = cr if rest is None else jnp.maximum(rest, cr)
        return jnp.max(first), (None if rest is None else jnp.max(rest))

    def values(bufs, units, step):
        hr = bufs.head_rows
        for u in units:
            k0 = key_start(u, step)
            for j in range(n_pairs):
                cols = slice(j * LANES, (j + 1) * LANES)
                v2 = v_ref[0, pl.ds(k0, CHUNK), cols]
                o2 = jnp.dot(bufs.att[pair_slice(bufs, u, j)], v2, preferred_element_type=F32)
                acc_ref[u * CHUNK:u * CHUNK + hr, cols] += jnp.where(first_head(bufs), o2[:hr], o2[hr:])

    def move_narrow_carries(to_narrow):
        for u in all_units:
            for j in range(n_pairs):
                fb = pair_slice(full, u, j).start
                nb = pair_slice(narrow, u, j).start
                for h in range(2):
                    f_rows = slice(fb + h * CHUNK, fb + h * CHUNK + NARROW_ROWS)
                    n_rows = slice(nb + h * NARROW_ROWS, nb + (h + 1) * NARROW_ROWS)
                    if to_narrow:
                        carry_n[n_rows] = carry_f[f_rows]
                    else:
                        carry_f[f_rows] = carry_n[n_rows]

    def unit0_has_keys(step):
        return qi * n_units - step >= 0

    scores(full, all_units, 0, True)
    c_first, c_rest = weights(full, all_units, first_step=True)

    def full_cond(state):
        step, _, c_rest = state
        return jnp.logical_and(unit0_has_keys(step), c_rest > LOG_WEIGHT_FLOOR)

    def full_body(state):
        step = state[0]
        scores(full, all_units, step, False)
        values(full, all_units, step - 1)
        c_first, c_rest = weights(full, all_units)
        return step + 1, c_first, c_rest

    step, c_first, c_rest = lax.while_loop(full_cond, full_body, (1, c_first, c_rest))
    values(full, all_units, step - 1)

    move_narrow_carries(True)

    def narrow_cond(state):
        step, c_first = state
        return jnp.logical_and(unit0_has_keys(step), c_first > LOG_WEIGHT_FLOOR)

    def narrow_body(state):
        step = state[0]
        scores(narrow, all_units, step, False)
        c_first, _ = weights(narrow, all_units)
        values(narrow, all_units, step)
        return step + 1, c_first

    step, c_first = lax.while_loop(narrow_cond, narrow_body, (step, c_first))

    cm = jnp.maximum(c_first, c_rest)

    @pl.when(jnp.logical_not(unit0_has_keys(step)))
    def _():
        move_narrow_carries(False)

    for u in all_units[1:]:
        def cond(state, u=u):
            step, cm = state
            return jnp.logical_and(qi * n_units + u - step >= 0, cm > LOG_WEIGHT_FLOOR)

        def body(state, u=u):
            step = state[0]
            scores(full, (u,), step, False)
            c_first, c_rest = weights(full, (u,))
            values(full, (u,), step)
            return step + 1, jnp.maximum(c_first, c_rest)

        lax.while_loop(cond, body, (step, cm))
    o_ref[0] = acc_ref[...]


def _attention(q, k, v, mcat):
    b, s, w = q.shape
    n_heads = w // HEAD_DIM
    n_units = ATTN_UNITS
    tq = n_units * CHUNK

    def sweep_scratch(head_rows):
        rows = n_units * n_heads * head_rows
        return [pltpu.VMEM((rows, LANES), BF16),
                pltpu.VMEM((rows, LANES), F32),
                pltpu.VMEM((rows, LANES), BF16),
                pltpu.VMEM((rows, LANES), BF16),
                pltpu.VMEM((rows, LANES), F32)]

    return pl.pallas_call(
        functools.partial(_attn_kernel, n_heads=n_heads, n_units=n_units),
        grid=(b, s // tq),
        in_specs=[pl.BlockSpec((1, tq, w), lambda bi, qi: (bi, qi, 0)),
                  pl.BlockSpec((1, s, w), lambda bi, qi: (bi, 0, 0)),
                  pl.BlockSpec((1, s, w), lambda bi, qi: (bi, 0, 0)),
                  _const_spec(mcat.shape)],
        out_specs=pl.BlockSpec((1, tq, w), lambda bi, qi: (bi, qi, 0)),
        out_shape=jax.ShapeDtypeStruct((b, s, w), F32),
        scratch_shapes=sweep_scratch(CHUNK) + sweep_scratch(NARROW_ROWS) + [pltpu.VMEM((tq, w), F32)],
        compiler_params=pltpu.CompilerParams(dimension_semantics=("arbitrary", "arbitrary"),
                                             vmem_limit_bytes=VMEM_LIMIT),
        name="attention",
    )(q, k, v, mcat)


def _mixer_out_kernel(x_ref, a_ref, yb_ref, p_ref, pprev_ref, sg_ref, sw_ref, sb_ref, pw_ref, ps_ref,
                      mg_ref, wo_ref, gm_ref, o_ref, *, tiles_per_seq):
    i = pl.program_id(0)
    tm = x_ref.shape[0]
    w_a = a_ref.shape[1] // 2
    w_c = p_ref.shape[1]
    gmat = gm_ref[...]
    seq_tile = i % tiles_per_seq

    p = p_ref[...]
    pprev = jnp.where(seq_tile == 0, 0.0, pprev_ref[...])
    acc = jnp.concatenate([pprev, p], axis=0)
    group = lax.broadcasted_iota(jnp.int32, (tm, w_c), 1) // (w_c // len(POOL_WINDOWS))
    pos1 = (seq_tile * tm + lax.broadcasted_iota(jnp.int32, (tm, w_c), 0) + 1).astype(F32)
    win = jnp.zeros((tm, w_c), F32)
    cnt = jnp.ones((tm, w_c), F32)
    span = 1
    for g, wdw in enumerate(POOL_WINDOWS):
        while span < wdw:
            acc = acc + pltpu.roll(acc, span, 0)
            span *= 2
        win = jnp.where(group == g, acc[HALO:], win)
        cnt = jnp.where(group == g, jnp.minimum(pos1, float(wdw)), cnt)
    d = win / cnt - p
    yc = jnp.dot(d.astype(BF16), pw_ref[...], preferred_element_type=F32) * ps_ref[...]

    n_h = w_a // HEAD_DIM
    r = lax.broadcasted_iota(jnp.int32, (CHUNK, CHUNK), 0)
    cidx = lax.broadcasted_iota(jnp.int32, (CHUNK, CHUNK), 1)
    wm_all = jnp.concatenate([jnp.where(cidx <= r, sw_ref[h], 0.0) for h in range(n_h)], axis=1).astype(BF16)
    head_of_lane = lax.broadcasted_iota(jnp.int32, (CHUNK, w_a), 1) // HEAD_DIM

    a = a_ref[...]
    ga = 0.5 * a * (1.0 + lax.erf(a * (2.0 ** -0.5)))
    u = ga[:, :w_a]
    vn = _head_rmsnorm(ga[:, w_a:], sg_ref[...], gmat).astype(BF16)
    ya_parts = []
    for c in range(tm // CHUNK):
        vb = vn[c * CHUNK:(c + 1) * CHUNK]
        v_bd = jnp.concatenate([jnp.where(head_of_lane == h, vb, jnp.zeros_like(vb)) for h in range(n_h)], axis=0)
        s = sb_ref[...] + jnp.dot(wm_all, v_bd, preferred_element_type=F32)
        ya_parts.append(u[c * CHUNK:(c + 1) * CHUNK] * s)
    ya = jnp.concatenate(ya_parts, axis=0)

    y = jnp.concatenate([ya, yb_ref[...], yc], axis=1)
    yn = _head_rmsnorm(y, mg_ref[...], gmat).astype(BF16)
    o_ref[...] = x_ref[...] + jnp.dot(yn, wo_ref[...], preferred_element_type=F32)


def _mixer_out(x, a, yb, p, sg, sw, sb_full, pw_bd, ps, mg, wo_all, layer, gmat, seq_len):
    n, d = x.shape
    tm = ROW_TILE
    hb = tm // HALO
    row = lambda c: pl.BlockSpec((tm, c), lambda i: (i, 0))
    prev = pl.BlockSpec((HALO, p.shape[1]), lambda i: (jnp.maximum(i * hb - 1, 0), 0))
    return pl.pallas_call(
        functools.partial(_mixer_out_kernel, tiles_per_seq=seq_len // tm),
        grid=(n // tm,),
        in_specs=[row(d), row(a.shape[1]), row(yb.shape[1]), row(p.shape[1]), prev,
                  _const_spec(sg.shape), _const_spec(sw.shape), _const_spec(sb_full.shape),
                  _const_spec(pw_bd.shape), _const_spec(ps.shape), _const_spec(mg.shape),
                  _layer_spec(wo_all, layer), _const_spec(gmat.shape)],
        out_specs=row(d),
        out_shape=jax.ShapeDtypeStruct((n, d), F32),
        compiler_params=pltpu.CompilerParams(dimension_semantics=("arbitrary",), vmem_limit_bytes=VMEM_LIMIT),
        name="mixer_out",
    )(x, a, yb, p, p, sg, sw, sb_full, pw_bd, ps, mg, wo_all, gmat)


def _ffn_kernel(x_ref, g_ref, wup_ref, cw_ref, cb_ref, wdn_ref, fg_ref, o_ref, h_ref, z_ref, ztail_ref, act_ref,
                *, tiles_per_seq, final_norm):
    i = pl.program_id(0)
    tm = x_ref.shape[0]
    d_ff = wdn_ref.shape[0]
    fc = FFN_CHUNK
    n_fc = d_ff // fc
    h_ref[...] = _row_rmsnorm(x_ref[...], g_ref[...]).astype(BF16)

    @pl.when(i % tiles_per_seq == 0)
    def _():
        ztail_ref[...] = jnp.zeros_like(ztail_ref)

    def gate_up(ref, c):
        return jnp.concatenate([ref[:, c * fc:(c + 1) * fc], ref[:, d_ff + c * fc:d_ff + (c + 1) * fc]], axis=1)

    def up_proj(c, buf):
        z = jnp.dot(h_ref[...], gate_up(wup_ref, c), preferred_element_type=F32)
        z_ref[buf, HALO - SUBLANES:HALO] = ztail_ref[c]
        z_ref[buf, HALO:] = z
        ztail_ref[c] = z[tm - SUBLANES:]

    def conv_gate(c, buf):
        zb = z_ref.at[buf]
        cw = gate_up(cw_ref, c)
        zc = (gate_up(cb_ref, c) + zb[HALO - 2:HALO - 2 + tm] * cw[0:1] + zb[HALO - 1:HALO - 1 + tm] * cw[1:2]
              + zb[HALO:] * cw[2:3])
        gate = zc[:, :fc]
        act_ref[:, c * fc:(c + 1) * fc] = (gate * jax.nn.sigmoid(gate) * zc[:, fc:]).astype(BF16)

    up_proj(0, 0)
    for c in range(n_fc):
        if c + 1 < n_fc:
            up_proj(c + 1, (c + 1) % 2)
        conv_gate(c, c % 2)

    out = x_ref[...]
    for k0 in range(0, d_ff, MXU_N):
        out = out + jnp.dot(act_ref[:, k0:k0 + MXU_N], wdn_ref[k0:k0 + MXU_N], preferred_element_type=F32)
    if final_norm:
        out = _row_rmsnorm(out, fg_ref[...])
    o_ref[...] = out


def _ffn(x, g, wup_all, cw, cb, wdn_all, layer, fg, seq_len, final_norm):
    n, d = x.shape
    tm = FFN_ROW_TILE
    d_ff = wdn_all.shape[1]
    row = pl.BlockSpec((tm, d), lambda i: (i, 0))
    return pl.pallas_call(
        functools.partial(_ffn_kernel, tiles_per_seq=seq_len // tm, final_norm=final_norm),
        grid=(n // tm,),
        in_specs=[row, _const_spec(g.shape), _layer_spec(wup_all, layer), _const_spec(cw.shape),
                  _const_spec(cb.shape), _layer_spec(wdn_all, layer), _const_spec(fg.shape)],
        out_specs=row,
        out_shape=jax.ShapeDtypeStruct((n, d), F32),
        scratch_shapes=[pltpu.VMEM((tm, d), BF16),
                        pltpu.VMEM((2, HALO + tm, 2 * FFN_CHUNK), F32),
                        pltpu.VMEM((d_ff // FFN_CHUNK, SUBLANES, 2 * FFN_CHUNK), F32),
                        pltpu.VMEM((tm, d_ff), BF16)],
        compiler_params=pltpu.CompilerParams(dimension_semantics=("arbitrary",), vmem_limit_bytes=VMEM_LIMIT),
        name="ffn",
    )(x, g, wup_all, cw, cb, wdn_all, fg)


def _constants():
    j = jnp.arange(CHUNK)
    tail = (j[:, None] >= j[None, :]).astype(BF16)
    mcat = -jnp.concatenate([tail, jnp.ones((CHUNK, LANES), BF16)], axis=1)
    i = jnp.arange(MXU_N)
    gmat = jnp.where(i[:, None] // HEAD_DIM == i[None, :] // HEAD_DIM, 1.0 / HEAD_DIM, 0.0).astype(BF16)
    return mcat, gmat


def kernel(x, norm1_g, w_in, sgu_norm_g, sgu_w, sgu_b, pool_w, pool_scale, mix_norm_g, w_o, norm2_g, w_up, conv_w,
           conv_b, w_down, final_g):
    b, s, d = x.shape
    depth = w_in.shape[0]
    w_a = sgu_norm_g.shape[1]
    w_c = pool_scale.shape[1]
    w_b = (w_in.shape[2] - 2 * w_a - w_c) // 3
    assert w_down.shape[1] % FFN_CHUNK == 0 and s % ROW_TILE == 0 and w_a + w_b + w_c == d
    mcat, gmat = _constants()
    w_in_b, w_o_b, w_up_b, w_down_b = (w.astype(BF16) for w in (w_in, w_o, w_up, w_down))

    xf = x.reshape(b * s, d)
    for l in range(depth):
        a, q, k, v, p = _in_proj(xf, norm1_g[l][None], w_in_b, l, 2 * w_a, w_b, w_c)
        yb = _attention(q.reshape(b, s, w_b), k.reshape(b, s, w_b), v.reshape(b, s, w_b), mcat)
        sb_full = jnp.repeat(sgu_b[l].T, HEAD_DIM, axis=1)
        pw_bd = jax.scipy.linalg.block_diag(*[pool_w[l, g] for g in range(pool_w.shape[1])]).astype(BF16)
        x1 = _mixer_out(xf, a, yb.reshape(b * s, w_b), p, sgu_norm_g[l][None], sgu_w[l], sb_full, pw_bd,
                        pool_scale[l][None], mix_norm_g[l][None], w_o_b, l, gmat, s)
        xf = _ffn(x1, norm2_g[l][None], w_up_b, conv_w[l], conv_b[l][None], w_down_b, l, final_g[None], s,
                  l == depth - 1)
    return xf.reshape(b, s, d)
```

```python
import functools
from typing import Any, NamedTuple

import jax
import jax.numpy as jnp
from jax import lax
from jax.experimental import pallas as pl
from jax.experimental.pallas import tpu as pltpu

F32 = jnp.float32
BF16 = jnp.bfloat16

EPS = 1e-6
HEAD_DIM = 64
CHUNK = 128
POOL_WINDOWS = (2, 4, 8, 16)
HALO = 16
LANES = 128
SUBLANES = 8
MXU_N = 256
FFN_CHUNK = MXU_N
ROW_TILE = 1024
FFN_ROW_TILE = 512
ATTN_UNITS = 4
NARROW_ROWS = 64
VMEM_LIMIT = 56 * 1024 * 1024
LOG_WEIGHT_FLOOR = -104.0
MASKED_SCORE = -1e30


def _row_rmsnorm(x, g):
    return x * lax.rsqrt(jnp.mean(x * x, axis=-1, keepdims=True) + EPS) * g


def _head_rmsnorm(y, g, gmat):
    y2 = (y * y).astype(BF16)
    parts = []
    for c in range(y.shape[1] // MXU_N):
        parts.append(jnp.dot(y2[:, c * MXU_N:(c + 1) * MXU_N], gmat, preferred_element_type=F32))
    ms = parts[0] if len(parts) == 1 else jnp.concatenate(parts, axis=1)
    return y * lax.rsqrt(ms + EPS) * g


def _const_spec(shape):
    return pl.BlockSpec(shape, lambda *_: (0,) * len(shape))


def _layer_spec(stacked, layer):
    return pl.BlockSpec((None,) + stacked.shape[1:], lambda *_: (layer,) + (0,) * (stacked.ndim - 1))


def _in_proj_kernel(x_ref, g_ref, w_ref, a_ref, q_ref, k_ref, v_ref, p_ref, *, w_a2, w_b):
    h = _row_rmsnorm(x_ref[...], g_ref[...]).astype(BF16)

    def proj(c0, c1):
        return jnp.dot(h, w_ref[:, c0:c1], preferred_element_type=F32)

    a_ref[...] = proj(0, w_a2)
    q_ref[...] = (proj(w_a2, w_a2 + w_b) * (HEAD_DIM ** -0.5)).astype(BF16)
    k_ref[...] = proj(w_a2 + w_b, w_a2 + 2 * w_b).astype(BF16)
    v_ref[...] = proj(w_a2 + 2 * w_b, w_a2 + 3 * w_b).astype(BF16)
    p_ref[...] = proj(w_a2 + 3 * w_b, w_ref.shape[1])


def _in_proj(x, g, w_all, layer, w_a2, w_b, w_c):
    n, d = x.shape
    tm = ROW_TILE
    row = lambda c: pl.BlockSpec((tm, c), lambda i: (i, 0))
    return pl.pallas_call(
        functools.partial(_in_proj_kernel, w_a2=w_a2, w_b=w_b),
        grid=(n // tm,),
        in_specs=[row(d), _const_spec((1, d)), _layer_spec(w_all, layer)],
        out_specs=[row(w_a2), row(w_b), row(w_b), row(w_b), row(w_c)],
        out_shape=[jax.ShapeDtypeStruct((n, w_a2), F32),
                   jax.ShapeDtypeStruct((n, w_b), BF16),
                   jax.ShapeDtypeStruct((n, w_b), BF16),
                   jax.ShapeDtypeStruct((n, w_b), BF16),
                   jax.ShapeDtypeStruct((n, w_c), F32)],
        compiler_params=pltpu.CompilerParams(dimension_semantics=("arbitrary",), vmem_limit_bytes=VMEM_LIMIT),
        name="in_proj",
    )(x, g, w_all)


class _SweepBuffers(NamedTuple):
    qs: Any
    z: Any
    l: Any
    att: Any
    carry: Any
    head_rows: int


def _attn_kernel(q_ref, k_ref, v_ref, m_ref, o_ref, qs_f, z_f, l_f, att_f, carry_f, qs_n, z_n, l_n, att_n, carry_n,
                 acc_ref, *, n_heads, n_units):
    qi = pl.program_id(1)
    n_pairs = n_heads // 2
    full = _SweepBuffers(qs_f, z_f, l_f, att_f, carry_f, CHUNK)
    narrow = _SweepBuffers(qs_n, z_n, l_n, att_n, carry_n, NARROW_ROWS)
    nt_dims = (((1,), (1,)), ((), ()))
    all_units = tuple(range(n_units))
    lane = lax.broadcasted_iota(jnp.int32, (2 * CHUNK, LANES), 1)
    row = lax.broadcasted_iota(jnp.int32, (2 * CHUNK, LANES), 0)
    non_causal_bias = jnp.where(lane < (row % CHUNK), 0.0, MASKED_SCORE)

    def pair_slice(bufs, u, j):
        base = (u * n_pairs + j) * 2 * bufs.head_rows
        return slice(base, base + 2 * bufs.head_rows)

    def first_head(bufs):
        return lax.broadcasted_iota(jnp.int32, (bufs.head_rows, LANES), 1) < HEAD_DIM

    def key_start(u, step):
        return pl.multiple_of((qi * n_units + u - step) * CHUNK, CHUNK)

    for bufs in (full, narrow):
        hr = bufs.head_rows
        ln = lax.broadcasted_iota(jnp.int32, (2 * hr, LANES), 1)
        rw = lax.broadcasted_iota(jnp.int32, (2 * hr, LANES), 0)
        own_lanes = (ln < HEAD_DIM) == (rw < hr)
        for u in all_units:
            for j in range(n_pairs):
                q2 = q_ref[0, u * CHUNK:u * CHUNK + hr, j * LANES:(j + 1) * LANES]
                q4 = jnp.concatenate([q2, q2], axis=0)
                bufs.qs[pair_slice(bufs, u, j)] = jnp.where(own_lanes, q4, jnp.zeros_like(q4))
    acc_ref[...] = jnp.zeros_like(acc_ref)

    def scores(bufs, units, step, diagonal):
        for u in units:
            k0 = key_start(u, step)
            for j in range(n_pairs):
                rows = pair_slice(bufs, u, j)
                k2 = k_ref[0, pl.ds(k0, CHUNK), j * LANES:(j + 1) * LANES]
                z = lax.dot_general(bufs.qs[rows], k2, nt_dims, preferred_element_type=F32)
                if diagonal:
                    z = z + non_causal_bias
                softplus = jnp.maximum(z, 0.0) + jnp.log(1.0 + jnp.exp(-jnp.abs(z)))
                bufs.z[rows] = z
                bufs.l[rows] = softplus.astype(BF16)

    def weights(bufs, units, first_step=False):
        hr = bufs.head_rows
        first = rest = None
        for u in units:
            for j in range(n_pairs):
                rows = pair_slice(bufs, u, j)
                res = jnp.dot(bufs.l[rows], m_ref[...], preferred_element_type=F32)
                log_att = bufs.z[rows] + res[:, :LANES]
                c = res[:, LANES:]
                if not first_step:
                    c_in = bufs.carry[rows]
                    log_att = log_att + c_in
                    c = c + c_in
                bufs.att[rows] = jnp.exp(log_att).astype(BF16)
                bufs.carry[rows] = c
                for h0 in (0, hr):
                    cf = c[h0:h0 + NARROW_ROWS]
                    first = cf if first is None else jnp.maximum(first, cf)
                    if hr > NARROW_ROWS:
                        cr = c[h0 + NARROW_ROWS:h0 + hr]
                        rest = cr if rest is None else jnp.maximum(rest, cr)
        return jnp.max(first), (None if rest is None else jnp.max(rest))

    def values(bufs, units, step):
        hr = bufs.head_rows
        for u in units:
            k0 = key_start(u, step)
            for j in range(n_pairs):
                cols = slice(j * LANES, (j + 1) * LANES)
                v2 = v_ref[0, pl.ds(k0, CHUNK), cols]
                o2 = jnp.dot(bufs.att[pair_slice(bufs, u, j)], v2, preferred_element_type=F32)
                acc_ref[u * CHUNK:u * CHUNK + hr, cols] += jnp.where(first_head(bufs), o2[:hr], o2[hr:])

    def move_narrow_carries(to_narrow):
        for u in all_units:
            for j in range(n_pairs):
                fb = pair_slice(full, u, j).start
                nb = pair_slice(narrow, u, j).start
                for h in range(2):
                    f_rows = slice(fb + h * CHUNK, fb + h * CHUNK + NARROW_ROWS)
                    n_rows = slice(nb + h * NARROW_ROWS, nb + (h + 1) * NARROW_ROWS)
                    if to_narrow:
                        carry_n[n_rows] = carry_f[f_rows]
                    else:
                        carry_f[f_rows] = carry_n[n_rows]

    def unit0_has_keys(step):
        return qi * n_units - step >= 0

    scores(full, all_units, 0, True)
    c_first, c_rest = weights(full, all_units, first_step=True)

    def full_cond(state):
        step, _, c_rest = state
        return jnp.logical_and(unit0_has_keys(step), c_rest > LOG_WEIGHT_FLOOR)

    def full_body(state):
        step = state[0]
        scores(full, all_units, step, False)
        values(full, all_units, step - 1)
        c_first, c_rest = weights(full, all_units)
        return step + 1, c_first, c_rest

    step, c_first, c_rest = lax.while_loop(full_cond, full_body, (1, c_first, c_rest))
    values(full, all_units, step - 1)

    move_narrow_carries(True)

    def narrow_cond(state):
        step, c_first = state
        return jnp.logical_and(unit0_has_keys(step), c_first > LOG_WEIGHT_FLOOR)

    def narrow_body(state):
        step = state[0]
        scores(narrow, all_units, step, False)
        c_first, _ = weights(narrow, all_units)
        values(narrow, all_units, step)
        return step + 1, c_first

    step, c_first = lax.while_loop(narrow_cond, narrow_body, (step, c_first))

    cm = jnp.maximum(c_first, c_rest)

    @pl.when(jnp.logical_not(unit0_has_keys(step)))
    def _():
        move_narrow_carries(False)

    for u in all_units[1:]:
        def cond(state, u=u):
            step, cm = state
            return jnp.logical_and(qi * n_units + u - step >= 0, cm > LOG_WEIGHT_FLOOR)

        def body(state, u=u):
            step = state[0]
            scores(full, (u,), step, False)
            c_first, c_rest = weights(full, (u,))
            values(full, (u,), step)
            return step + 1, jnp.maximum(c_first, c_rest)

        lax.while_loop(cond, body, (step, cm))
    o_ref[0] = acc_ref[...]


def _attention(q, k, v, mcat):
    b, s, w = q.shape
    n_heads = w // HEAD_DIM
    n_units = ATTN_UNITS
    tq = n_units * CHUNK

    def sweep_scratch(head_rows):
        rows = n_units * n_heads * head_rows
        return [pltpu.VMEM((rows, LANES), BF16),
                pltpu.VMEM((rows, LANES), F32),
                pltpu.VMEM((rows, LANES), BF16),
                pltpu.VMEM((rows, LANES), BF16),
                pltpu.VMEM((rows, LANES), F32)]

    return pl.pallas_call(
        functools.partial(_attn_kernel, n_heads=n_heads, n_units=n_units),
        grid=(b, s // tq),
        in_specs=[pl.BlockSpec((1, tq, w), lambda bi, qi: (bi, qi, 0)),
                  pl.BlockSpec((1, s, w), lambda bi, qi: (bi, 0, 0)),
                  pl.BlockSpec((1, s, w), lambda bi, qi: (bi, 0, 0)),
                  _const_spec(mcat.shape)],
        out_specs=pl.BlockSpec((1, tq, w), lambda bi, qi: (bi, qi, 0)),
        out_shape=jax.ShapeDtypeStruct((b, s, w), F32),
        scratch_shapes=sweep_scratch(CHUNK) + sweep_scratch(NARROW_ROWS) + [pltpu.VMEM((tq, w), F32)],
        compiler_params=pltpu.CompilerParams(dimension_semantics=("arbitrary", "arbitrary"),
                                             vmem_limit_bytes=VMEM_LIMIT),
        name="attention",
    )(q, k, v, mcat)


def _mixer_out_kernel(x_ref, a_ref, yb_ref, p_ref, pprev_ref, sg_ref, sw_ref, sb_ref, pw_ref, ps_ref,
                      mg_ref, wo_ref, gm_ref, o_ref, *, tiles_per_seq):
    i = pl.program_id(0)
    tm = x_ref.shape[0]
    w_a = a_ref.shape[1] // 2
    w_c = p_ref.shape[1]
    gmat = gm_ref[...]
    seq_tile = i % tiles_per_seq

    p = p_ref[...]
    pprev = jnp.where(seq_tile == 0, 0.0, pprev_ref[...])
    acc = jnp.concatenate([pprev, p], axis=0)
    group = lax.broadcasted_iota(jnp.int32, (tm, w_c), 1) // (w_c // len(POOL_WINDOWS))
    pos1 = (seq_tile * tm + lax.broadcasted_iota(jnp.int32, (tm, w_c), 0) + 1).astype(F32)
    win = jnp.zeros((tm, w_c), F32)
    cnt = jnp.ones((tm, w_c), F32)
    span = 1
    for g, wdw in enumerate(POOL_WINDOWS):
        while span < wdw:
            acc = acc + pltpu.roll(acc, span, 0)
            span *= 2
        win = jnp.where(group == g, acc[HALO:], win)
        cnt = jnp.where(group == g, jnp.minimum(pos1, float(wdw)), cnt)
    d = win / cnt - p
    yc = jnp.dot(d.astype(BF16), pw_ref[...], preferred_element_type=F32) * ps_ref[...]

    n_h = w_a // HEAD_DIM
    r = lax.broadcasted_iota(jnp.int32, (CHUNK, CHUNK), 0)
    cidx = lax.broadcasted_iota(jnp.int32, (CHUNK, CHUNK), 1)
    wm_all = jnp.concatenate([jnp.where(cidx <= r, sw_ref[h], 0.0) for h in range(n_h)], axis=1).astype(BF16)
    head_of_lane = lax.broadcasted_iota(jnp.int32, (CHUNK, w_a), 1) // HEAD_DIM

    a = a_ref[...]
    ga = 0.5 * a * (1.0 + lax.erf(a * (2.0 ** -0.5)))
    u = ga[:, :w_a]
    vn = _head_rmsnorm(ga[:, w_a:], sg_ref[...], gmat).astype(BF16)
    ya_parts = []
    for c in range(tm // CHUNK):
        vb = vn[c * CHUNK:(c + 1) * CHUNK]
        v_bd = jnp.concatenate([jnp.where(head_of_lane == h, vb, jnp.zeros_like(vb)) for h in range(n_h)], axis=0)
        s = sb_ref[...] + jnp.dot(wm_all, v_bd, preferred_element_type=F32)
        ya_parts.append(u[c * CHUNK:(c + 1) * CHUNK] * s)
    ya = jnp.concatenate(ya_parts, axis=0)

    y = jnp.concatenate([ya, yb_ref[...], yc], axis=1)
    yn = _head_rmsnorm(y, mg_ref[...], gmat).astype(BF16)
    o_ref[...] = x_ref[...] + jnp.dot(yn, wo_ref[...], preferred_element_type=F32)


def _mixer_out(x, a, yb, p, sg, sw, sb_full, pw_bd, ps, mg, wo_all, layer, gmat, seq_len):
    n, d = x.shape
    tm = ROW_TILE
    hb = tm // HALO
    row = lambda c: pl.BlockSpec((tm, c), lambda i: (i, 0))
    prev = pl.BlockSpec((HALO, p.shape[1]), lambda i: (jnp.maximum(i * hb - 1, 0), 0))
    return pl.pallas_call(
        functools.partial(_mixer_out_kernel, tiles_per_seq=seq_len // tm),
        grid=(n // tm,),
        in_specs=[row(d), row(a.shape[1]), row(yb.shape[1]), row(p.shape[1]), prev,
                  _const_spec(sg.shape), _const_spec(sw.shape), _const_spec(sb_full.shape),
                  _const_spec(pw_bd.shape), _const_spec(ps.shape), _const_spec(mg.shape),
                  _layer_spec(wo_all, layer), _const_spec(gmat.shape)],
        out_specs=row(d),
        out_shape=jax.ShapeDtypeStruct((n, d), F32),
        compiler_params=pltpu.CompilerParams(dimension_semantics=("arbitrary",), vmem_limit_bytes=VMEM_LIMIT),
        name="mixer_out",
    )(x, a, yb, p, p, sg, sw, sb_full, pw_bd, ps, mg, wo_all, gmat)


def _ffn_kernel(x_ref, g_ref, wup_ref, cw_ref, cb_ref, wdn_ref, fg_ref, o_ref, h_ref, z_ref, ztail_ref, act_ref,
                *, tiles_per_seq, final_norm):
    i = pl.program_id(0)
    tm = x_ref.shape[0]
    d_ff = wdn_ref.shape[0]
    fc = FFN_CHUNK
    n_fc = d_ff // fc
    h_ref[...] = _row_rmsnorm(x_ref[...], g_ref[...]).astype(BF16)

    @pl.when(i % tiles_per_seq == 0)
    def _():
        ztail_ref[...] = jnp.zeros_like(ztail_ref)

    def gate_up(ref, c):
        return jnp.concatenate([ref[:, c * fc:(c + 1) * fc], ref[:, d_ff + c * fc:d_ff + (c + 1) * fc]], axis=1)

    def up_proj(c, buf):
        z = jnp.dot(h_ref[...], gate_up(wup_ref, c), preferred_element_type=F32)
        z_ref[buf, HALO - SUBLANES:HALO] = ztail_ref[c]
        z_ref[buf, HALO:] = z
        ztail_ref[c] = z[tm - SUBLANES:]

    def conv_gate(c, buf):
        zb = z_ref.at[buf]
        cw = gate_up(cw_ref, c)
        zc = (gate_up(cb_ref, c) + zb[HALO - 2:HALO - 2 + tm] * cw[0:1] + zb[HALO - 1:HALO - 1 + tm] * cw[1:2]
              + zb[HALO:] * cw[2:3])
        gate = zc[:, :fc]
        act_ref[:, c * fc:(c + 1) * fc] = (gate * jax.nn.sigmoid(gate) * zc[:, fc:]).astype(BF16)

    up_proj(0, 0)
    for c in range(n_fc):
        if c + 1 < n_fc:
            up_proj(c + 1, (c + 1) % 2)
        conv_gate(c, c % 2)

    out = x_ref[...]
    for k0 in range(0, d_ff, MXU_N):
        out = out + jnp.dot(act_ref[:, k0:k0 + MXU_N], wdn_ref[k0:k0 + MXU_N], preferred_element_type=F32)
    if final_norm:
        out = _row_rmsnorm(out, fg_ref[...])
    o_ref[...] = out


def _ffn(x, g, wup_all, cw, cb, wdn_all, layer, fg, seq_len, final_norm):
    n, d = x.shape
    tm = FFN_ROW_TILE
    d_ff = wdn_all.shape[1]
    row = pl.BlockSpec((tm, d), lambda i: (i, 0))
    return pl.pallas_call(
        functools.partial(_ffn_kernel, tiles_per_seq=seq_len // tm, final_norm=final_norm),
        grid=(n // tm,),
        in_specs=[row, _const_spec(g.shape), _layer_spec(wup_all, layer), _const_spec(cw.shape),
                  _const_spec(cb.shape), _layer_spec(wdn_all, layer), _const_spec(fg.shape)],
        out_specs=row,
        out_shape=jax.ShapeDtypeStruct((n, d), F32),
        scratch_shapes=[pltpu.VMEM((tm, d), BF16),
                        pltpu.VMEM((2, HALO + tm, 2 * FFN_CHUNK), F32),
                        pltpu.VMEM((d_ff // FFN_CHUNK, SUBLANES, 2 * FFN_CHUNK), F32),
                        pltpu.VMEM((tm, d_ff), BF16)],
        compiler_params=pltpu.CompilerParams(dimension_semantics=("arbitrary",), vmem_limit_bytes=VMEM_LIMIT),
        name="ffn",
    )(x, g, wup_all, cw, cb, wdn_all, fg)


def _constants():
    j = jnp.arange(CHUNK)
    tail = (j[:, None] >= j[None, :]).astype(BF16)
    mcat = -jnp.concatenate([tail, jnp.ones((CHUNK, LANES), BF16)], axis=1)
    i = jnp.arange(MXU_N)
    gmat = jnp.where(i[:, None] // HEAD_DIM == i[None, :] // HEAD_DIM, 1.0 / HEAD_DIM, 0.0).astype(BF16)
    return mcat, gmat


def kernel(x, norm1_g, w_in, sgu_norm_g, sgu_w, sgu_b, pool_w, pool_scale, mix_norm_g, w_o, norm2_g, w_up, conv_w,
           conv_b, w_down, final_g):
    b, s, d = x.shape
    depth = w_in.shape[0]
    w_a = sgu_norm_g.shape[1]
    w_c = pool_scale.shape[1]
    w_b = (w_in.shape[2] - 2 * w_a - w_c) // 3
    assert w_down.shape[1] % FFN_CHUNK == 0 and s % ROW_TILE == 0 and w_a + w_b + w_c == d
    mcat, gmat = _constants()
    w_in_b, w_o_b, w_up_b, w_down_b = (w.astype(BF16) for w in (w_in, w_o, w_up, w_down))

    xf = x.reshape(b * s, d)
    for l in range(depth):
        a, q, k, v, p = _in_proj(xf, norm1_g[l][None], w_in_b, l, 2 * w_a, w_b, w_c)
        yb = _attention(q.reshape(b, s, w_b), k.reshape(b, s, w_b), v.reshape(b, s, w_b), mcat)
        sb_full = jnp.repeat(sgu_b[l].T, HEAD_DIM, axis=1)
        pw_bd = jax.scipy.linalg.block_diag(*[pool_w[l, g] for g in range(pool_w.shape[1])]).astype(BF16)
        x1 = _mixer_out(xf, a, yb.reshape(b * s, w_b), p, sgu_norm_g[l][None], sgu_w[l], sb_full, pw_bd,
                        pool_scale[l][None], mix_norm_g[l][None], w_o_b, l, gmat, s)
        xf = _ffn(x1, norm2_g[l][None], w_up_b, conv_w[l], conv_b[l][None], w_down_b, l, final_g[None], s,
                  l == depth - 1)
    return xf.reshape(b, s, d)
```

```python
import functools
from typing import Any, NamedTuple

import jax
import jax.numpy as jnp
from jax import lax
from jax.experimental import pallas as pl
from jax.experimental.pallas import tpu as pltpu

F32 = jnp.float32
BF16 = jnp.bfloat16

EPS = 1e-6
HEAD_DIM = 64
CHUNK = 128
POOL_WINDOWS = (2, 4, 8, 16)
HALO = 16
LANES = 128
SUBLANES = 8
MXU_N = 256
FFN_CHUNK = MXU_N
ROW_TILE = 1024
FFN_ROW_TILE = 512
ATTN_UNITS = 4
NARROW_ROWS = 64
VMEM_LIMIT = 56 * 1024 * 1024
LOG_WEIGHT_FLOOR = -104.0
NEG_LOG2_E = -1.4426950408889634
MASKED_SCORE = -1e30


def _row_rmsnorm(x, g):
    return x * lax.rsqrt(jnp.mean(x * x, axis=-1, keepdims=True) + EPS) * g


def _head_rmsnorm(y, g, gmat):
    y2 = (y * y).astype(BF16)
    parts = []
    for c in range(y.shape[1] // MXU_N):
        parts.append(jnp.dot(y2[:, c * MXU_N:(c + 1) * MXU_N], gmat, preferred_element_type=F32))
    ms = parts[0] if len(parts) == 1 else jnp.concatenate(parts, axis=1)
    return y * lax.rsqrt(ms + EPS) * g


def _const_spec(shape):
    return pl.BlockSpec(shape, lambda *_: (0,) * len(shape))


def _layer_spec(stacked, layer):
    return pl.BlockSpec((None,) + stacked.shape[1:], lambda *_: (layer,) + (0,) * (stacked.ndim - 1))


def _in_proj_kernel(x_ref, g_ref, w_ref, a_ref, q_ref, k_ref, v_ref, p_ref, *, w_a2, w_b):
    h = _row_rmsnorm(x_ref[...], g_ref[...]).astype(BF16)

    def proj(c0, c1):
        return jnp.dot(h, w_ref[:, c0:c1], preferred_element_type=F32)

    a_ref[...] = proj(0, w_a2)
    q_ref[...] = (proj(w_a2, w_a2 + w_b) * (HEAD_DIM ** -0.5)).astype(BF16)
    k_ref[...] = proj(w_a2 + w_b, w_a2 + 2 * w_b).astype(BF16)
    v_ref[...] = proj(w_a2 + 2 * w_b, w_a2 + 3 * w_b).astype(BF16)
    p_ref[...] = proj(w_a2 + 3 * w_b, w_ref.shape[1])


def _in_proj(x, g, w_all, layer, w_a2, w_b, w_c):
    n, d = x.shape
    tm = ROW_TILE
    row = lambda c: pl.BlockSpec((tm, c), lambda i: (i, 0))
    return pl.pallas_call(
        functools.partial(_in_proj_kernel, w_a2=w_a2, w_b=w_b),
        grid=(n // tm,),
        in_specs=[row(d), _const_spec((1, d)), _layer_spec(w_all, layer)],
        out_specs=[row(w_a2), row(w_b), row(w_b), row(w_b), row(w_c)],
        out_shape=[jax.ShapeDtypeStruct((n, w_a2), F32),
                   jax.ShapeDtypeStruct((n, w_b), BF16),
                   jax.ShapeDtypeStruct((n, w_b), BF16),
                   jax.ShapeDtypeStruct((n, w_b), BF16),
                   jax.ShapeDtypeStruct((n, w_c), F32)],
        compiler_params=pltpu.CompilerParams(dimension_semantics=("arbitrary",), vmem_limit_bytes=VMEM_LIMIT),
        name="in_proj",
    )(x, g, w_all)


class _SweepBuffers(NamedTuple):
    qs: Any
    z: Any
    l: Any
    att: Any
    carry: Any
    head_rows: int


def _attn_kernel(q_ref, k_ref, v_ref, m_ref, o_ref, qs_f, z_f, l_f, att_f, carry_f, qs_n, z_n, l_n, att_n, carry_n,
                 acc_ref, *, n_heads, n_units):
    qi = pl.program_id(1)
    n_pairs = n_heads // 2
    full = _SweepBuffers(qs_f, z_f, l_f, att_f, carry_f, CHUNK)
    narrow = _SweepBuffers(qs_n, z_n, l_n, att_n, carry_n, NARROW_ROWS)
    nt_dims = (((1,), (1,)), ((), ()))
    all_units = tuple(range(n_units))
    lane = lax.broadcasted_iota(jnp.int32, (2 * CHUNK, LANES), 1)
    row = lax.broadcasted_iota(jnp.int32, (2 * CHUNK, LANES), 0)
    non_causal_bias = jnp.where(lane < (row % CHUNK), 0.0, MASKED_SCORE)

    def pair_slice(bufs, u, j):
        base = (u * n_pairs + j) * 2 * bufs.head_rows
        return slice(base, base + 2 * bufs.head_rows)

    def first_head(bufs):
        return lax.broadcasted_iota(jnp.int32, (bufs.head_rows, LANES), 1) < HEAD_DIM

    def key_start(u, step):
        return pl.multiple_of((qi * n_units + u - step) * CHUNK, CHUNK)

    for bufs in (full, narrow):
        hr = bufs.head_rows
        ln = lax.broadcasted_iota(jnp.int32, (2 * hr, LANES), 1)
        rw = lax.broadcasted_iota(jnp.int32, (2 * hr, LANES), 0)
        own_lanes = (ln < HEAD_DIM) == (rw < hr)
        for u in all_units:
            for j in range(n_pairs):
                q2 = q_ref[0, u * CHUNK:u * CHUNK + hr, j * LANES:(j + 1) * LANES]
                q4 = jnp.concatenate([q2, q2], axis=0)
                bufs.qs[pair_slice(bufs, u, j)] = jnp.where(own_lanes, q4, jnp.zeros_like(q4))
    acc_ref[...] = jnp.zeros_like(acc_ref)

    def scores(bufs, units, step, diagonal):
        for u in units:
            k0 = key_start(u, step)
            for j in range(n_pairs):
                rows = pair_slice(bufs, u, j)
                k2 = k_ref[0, pl.ds(k0, CHUNK), j * LANES:(j + 1) * LANES]
                z = lax.dot_general(bufs.qs[rows], k2, nt_dims, preferred_element_type=F32)
                if diagonal:
                    z = z + non_causal_bias
                softplus = jnp.maximum(z, 0.0) + jnp.log(1.0 + jnp.exp2(jnp.abs(z) * NEG_LOG2_E))
                bufs.z[rows] = z
                bufs.l[rows] = softplus.astype(BF16)

    def weights(bufs, units, first_step=False):
        hr = bufs.head_rows
        first = rest = None
        for u in units:
            for j in range(n_pairs):
                rows = pair_slice(bufs, u, j)
                res = jnp.dot(bufs.l[rows], m_ref[...], preferred_element_type=F32)
                log_att = bufs.z[rows] + res[:, :LANES]
                c = res[:, LANES:]
                if not first_step:
                    c_in = bufs.carry[rows]
                    log_att = log_att + c_in
                    c = c + c_in
                bufs.att[rows] = jnp.exp(log_att).astype(BF16)
                bufs.carry[rows] = c
                for h0 in (0, hr):
                    cf = c[h0:h0 + NARROW_ROWS]
                    first = cf if first is None else jnp.maximum(first, cf)
                    if hr > NARROW_ROWS:
                        cr = c[h0 + NARROW_ROWS:h0 + hr]
                        rest = cr if rest is None else jnp.maximum(rest, cr)
        return jnp.max(first), (None if rest is None else jnp.max(rest))

    def values(bufs, units, step):
        hr = bufs.head_rows
        for u in units:
            k0 = key_start(u, step)
            for j in range(n_pairs):
                cols = slice(j * LANES, (j + 1) * LANES)
                v2 = v_ref[0, pl.ds(k0, CHUNK), cols]
                o2 = jnp.dot(bufs.att[pair_slice(bufs, u, j)], v2, preferred_element_type=F32)
                acc_ref[u * CHUNK:u * CHUNK + hr, cols] += jnp.where(first_head(bufs), o2[:hr], o2[hr:])

    def move_narrow_carries(to_narrow):
        for u in all_units:
            for j in range(n_pairs):
                fb = pair_slice(full, u, j).start
                nb = pair_slice(narrow, u, j).start
                for h in range(2):
                    f_rows = slice(fb + h * CHUNK, fb + h * CHUNK + NARROW_ROWS)
                    n_rows = slice(nb + h * NARROW_ROWS, nb + (h + 1) * NARROW_ROWS)
                    if to_narrow:
                        carry_n[n_rows] = carry_f[f_rows]
                    else:
                        carry_f[f_rows] = carry_n[n_rows]

    def unit0_has_keys(step):
        return qi * n_units - step >= 0

    scores(full, all_units, 0, True)
    c_first, c_rest = weights(full, all_units, first_step=True)

    def full_cond(state):
        step, _, c_rest = state
        return jnp.logical_and(unit0_has_keys(step), c_rest > LOG_WEIGHT_FLOOR)

    def full_body(state):
        step = state[0]
        scores(full, all_units, step, False)
        values(full, all_units, step - 1)
        c_first, c_rest = weights(full, all_units)
        return step + 1, c_first, c_rest

    step, c_first, c_rest = lax.while_loop(full_cond, full_body, (1, c_first, c_rest))
    values(full, all_units, step - 1)

    move_narrow_carries(True)

    def narrow_cond(state):
        step, c_first = state
        return jnp.logical_and(unit0_has_keys(step), c_first > LOG_WEIGHT_FLOOR)

    def narrow_body(state):
        step = state[0]
        scores(narrow, all_units, step, False)
        c_first, _ = weights(narrow, all_units)
        values(narrow, all_units, step)
        return step + 1, c_first

    step, c_first = lax.while_loop(narrow_cond, narrow_body, (step, c_first))

    cm = jnp.maximum(c_first, c_rest)

    @pl.when(jnp.logical_not(unit0_has_keys(step)))
    def _():
        move_narrow_carries(False)

    for u in all_units[1:]:
        def cond(state, u=u):
            step, cm = state
            return jnp.logical_and(qi * n_units + u - step >= 0, cm > LOG_WEIGHT_FLOOR)

        def body(state, u=u):
            step = state[0]
            scores(full, (u,), step, False)
            c_first, c_rest = weights(full, (u,))
            values(full, (u,), step)
            return step + 1, jnp.maximum(c_first, c_rest)

        lax.while_loop(cond, body, (step, cm))
    o_ref[0] = acc_ref[...]


def _attention(q, k, v, mcat):
    b, s, w = q.shape
    n_heads = w // HEAD_DIM
    n_units = ATTN_UNITS
    tq = n_units * CHUNK

    def sweep_scratch(head_rows):
        rows = n_units * n_heads * head_rows
        return [pltpu.VMEM((rows, LANES), BF16),
                pltpu.VMEM((rows, LANES), F32),
                pltpu.VMEM((rows, LANES), BF16),
                pltpu.VMEM((rows, LANES), BF16),
                pltpu.VMEM((rows, LANES), F32)]

    return pl.pallas_call(
        functools.partial(_attn_kernel, n_heads=n_heads, n_units=n_units),
        grid=(b, s // tq),
        in_specs=[pl.BlockSpec((1, tq, w), lambda bi, qi: (bi, qi, 0)),
                  pl.BlockSpec((1, s, w), lambda bi, qi: (bi, 0, 0)),
                  pl.BlockSpec((1, s, w), lambda bi, qi: (bi, 0, 0)),
                  _const_spec(mcat.shape)],
        out_specs=pl.BlockSpec((1, tq, w), lambda bi, qi: (bi, qi, 0)),
        out_shape=jax.ShapeDtypeStruct((b, s, w), F32),
        scratch_shapes=sweep_scratch(CHUNK) + sweep_scratch(NARROW_ROWS) + [pltpu.VMEM((tq, w), F32)],
        compiler_params=pltpu.CompilerParams(dimension_semantics=("arbitrary", "arbitrary"),
                                             vmem_limit_bytes=VMEM_LIMIT),
        name="attention",
    )(q, k, v, mcat)


def _mixer_out_kernel(x_ref, a_ref, yb_ref, p_ref, pprev_ref, sg_ref, sw_ref, sb_ref, pw_ref, ps_ref,
                      mg_ref, wo_ref, gm_ref, o_ref, *, tiles_per_seq):
    i = pl.program_id(0)
    tm = x_ref.shape[0]
    w_a = a_ref.shape[1] // 2
    w_c = p_ref.shape[1]
    gmat = gm_ref[...]
    seq_tile = i % tiles_per_seq

    p = p_ref[...]
    pprev = jnp.where(seq_tile == 0, 0.0, pprev_ref[...])
    acc = jnp.concatenate([pprev, p], axis=0)
    group = lax.broadcasted_iota(jnp.int32, (tm, w_c), 1) // (w_c // len(POOL_WINDOWS))
    pos1 = (seq_tile * tm + lax.broadcasted_iota(jnp.int32, (tm, w_c), 0) + 1).astype(F32)
    win = jnp.zeros((tm, w_c), F32)
    cnt = jnp.ones((tm, w_c), F32)
    span = 1
    for g, wdw in enumerate(POOL_WINDOWS):
        while span < wdw:
            acc = acc + pltpu.roll(acc, span, 0)
            span *= 2
        win = jnp.where(group == g, acc[HALO:], win)
        cnt = jnp.where(group == g, jnp.minimum(pos1, float(wdw)), cnt)
    d = win / cnt - p
    yc = jnp.dot(d.astype(BF16), pw_ref[...], preferred_element_type=F32) * ps_ref[...]

    n_h = w_a // HEAD_DIM
    r = lax.broadcasted_iota(jnp.int32, (CHUNK, CHUNK), 0)
    cidx = lax.broadcasted_iota(jnp.int32, (CHUNK, CHUNK), 1)
    wm_all = jnp.concatenate([jnp.where(cidx <= r, sw_ref[h], 0.0) for h in range(n_h)], axis=1).astype(BF16)
    head_of_lane = lax.broadcasted_iota(jnp.int32, (CHUNK, w_a), 1) // HEAD_DIM

    a = a_ref[...]
    ga = 0.5 * a * (1.0 + lax.erf(a * (2.0 ** -0.5)))
    u = ga[:, :w_a]
    vn = _head_rmsnorm(ga[:, w_a:], sg_ref[...], gmat).astype(BF16)
    ya_parts = []
    for c in range(tm // CHUNK):
        vb = vn[c * CHUNK:(c + 1) * CHUNK]
        v_bd = jnp.concatenate([jnp.where(head_of_lane == h, vb, jnp.zeros_like(vb)) for h in range(n_h)], axis=0)
        s = sb_ref[...] + jnp.dot(wm_all, v_bd, preferred_element_type=F32)
        ya_parts.append(u[c * CHUNK:(c + 1) * CHUNK] * s)
    ya = jnp.concatenate(ya_parts, axis=0)

    y = jnp.concatenate([ya, yb_ref[...], yc], axis=1)
    yn = _head_rmsnorm(y, mg_ref[...], gmat).astype(BF16)
    o_ref[...] = x_ref[...] + jnp.dot(yn, wo_ref[...], preferred_element_type=F32)


def _mixer_out(x, a, yb, p, sg, sw, sb_full, pw_bd, ps, mg, wo_all, layer, gmat, seq_len):
    n, d = x.shape
    tm = ROW_TILE
    hb = tm // HALO
    row = lambda c: pl.BlockSpec((tm, c), lambda i: (i, 0))
    prev = pl.BlockSpec((HALO, p.shape[1]), lambda i: (jnp.maximum(i * hb - 1, 0), 0))
    return pl.pallas_call(
        functools.partial(_mixer_out_kernel, tiles_per_seq=seq_len // tm),
        grid=(n // tm,),
        in_specs=[row(d), row(a.shape[1]), row(yb.shape[1]), row(p.shape[1]), prev,
                  _const_spec(sg.shape), _const_spec(sw.shape), _const_spec(sb_full.shape),
                  _const_spec(pw_bd.shape), _const_spec(ps.shape), _const_spec(mg.shape),
                  _layer_spec(wo_all, layer), _const_spec(gmat.shape)],
        out_specs=row(d),
        out_shape=jax.ShapeDtypeStruct((n, d), F32),
        compiler_params=pltpu.CompilerParams(dimension_semantics=("arbitrary",), vmem_limit_bytes=VMEM_LIMIT),
        name="mixer_out",
    )(x, a, yb, p, p, sg, sw, sb_full, pw_bd, ps, mg, wo_all, gmat)


def _ffn_kernel(x_ref, g_ref, wup_ref, cw_ref, cb_ref, wdn_ref, fg_ref, o_ref, h_ref, z_ref, ztail_ref, act_ref,
                *, tiles_per_seq, final_norm):
    i = pl.program_id(0)
    tm = x_ref.shape[0]
    d_ff = wdn_ref.shape[0]
    fc = FFN_CHUNK
    n_fc = d_ff // fc
    h_ref[...] = _row_rmsnorm(x_ref[...], g_ref[...]).astype(BF16)

    @pl.when(i % tiles_per_seq == 0)
    def _():
        ztail_ref[...] = jnp.zeros_like(ztail_ref)

    def gate_up(ref, c):
        return jnp.concatenate([ref[:, c * fc:(c + 1) * fc], ref[:, d_ff + c * fc:d_ff + (c + 1) * fc]], axis=1)

    def up_proj(c, buf):
        z = jnp.dot(h_ref[...], gate_up(wup_ref, c), preferred_element_type=F32)
        z_ref[buf, HALO - SUBLANES:HALO] = ztail_ref[c]
        z_ref[buf, HALO:] = z
        ztail_ref[c] = z[tm - SUBLANES:]

    def conv_gate(c, buf):
        zb = z_ref.at[buf]
        cw = gate_up(cw_ref, c)
        zc = (gate_up(cb_ref, c) + zb[HALO - 2:HALO - 2 + tm] * cw[0:1] + zb[HALO - 1:HALO - 1 + tm] * cw[1:2]
              + zb[HALO:] * cw[2:3])
        gate = zc[:, :fc]
        act_ref[:, c * fc:(c + 1) * fc] = (gate * jax.nn.sigmoid(gate) * zc[:, fc:]).astype(BF16)

    up_proj(0, 0)
    for c in range(n_fc):
        if c + 1 < n_fc:
            up_proj(c + 1, (c + 1) % 2)
        conv_gate(c, c % 2)

    out = x_ref[...]
    for k0 in range(0, d_ff, MXU_N):
        out = out + jnp.dot(act_ref[:, k0:k0 + MXU_N], wdn_ref[k0:k0 + MXU_N], preferred_element_type=F32)
    if final_norm:
        out = _row_rmsnorm(out, fg_ref[...])
    o_ref[...] = out


def _ffn(x, g, wup_all, cw, cb, wdn_all, layer, fg, seq_len, final_norm):
    n, d = x.shape
    tm = FFN_ROW_TILE
    d_ff = wdn_all.shape[1]
    row = pl.BlockSpec((tm, d), lambda i: (i, 0))
    return pl.pallas_call(
        functools.partial(_ffn_kernel, tiles_per_seq=seq_len // tm, final_norm=final_norm),
        grid=(n // tm,),
        in_specs=[row, _const_spec(g.shape), _layer_spec(wup_all, layer), _const_spec(cw.shape),
                  _const_spec(cb.shape), _layer_spec(wdn_all, layer), _const_spec(fg.shape)],
        out_specs=row,
        out_shape=jax.ShapeDtypeStruct((n, d), F32),
        scratch_shapes=[pltpu.VMEM((tm, d), BF16),
                        pltpu.VMEM((2, HALO + tm, 2 * FFN_CHUNK), F32),
                        pltpu.VMEM((d_ff // FFN_CHUNK, SUBLANES, 2 * FFN_CHUNK), F32),
                        pltpu.VMEM((tm, d_ff), BF16)],
        compiler_params=pltpu.CompilerParams(dimension_semantics=("arbitrary",), vmem_limit_bytes=VMEM_LIMIT),
        name="ffn",
    )(x, g, wup_all, cw, cb, wdn_all, fg)


def _constants():
    j = jnp.arange(CHUNK)
    tail = (j[:, None] >= j[None, :]).astype(BF16)
    mcat = -jnp.concatenate([tail, jnp.ones((CHUNK, LANES), BF16)], axis=1)
    i = jnp.arange(MXU_N)
    gmat = jnp.where(i[:, None] // HEAD_DIM == i[None, :] // HEAD_DIM, 1.0 / HEAD_DIM, 0.0).astype(BF16)
    return mcat, gmat


def kernel(x, norm1_g, w_in, sgu_norm_g, sgu_w, sgu_b, pool_w, pool_scale, mix_norm_g, w_o, norm2_g, w_up, conv_w,
           conv_b, w_down, final_g):
    b, s, d = x.shape
    depth = w_in.shape[0]
    w_a = sgu_norm_g.shape[1]
    w_c = pool_scale.shape[1]
    w_b = (w_in.shape[2] - 2 * w_a - w_c) // 3
    assert w_down.shape[1] % FFN_CHUNK == 0 and s % ROW_TILE == 0 and w_a + w_b + w_c == d
    mcat, gmat = _constants()
    w_in_b, w_o_b, w_up_b, w_down_b = (w.astype(BF16) for w in (w_in, w_o, w_up, w_down))

    xf = x.reshape(b * s, d)
    for l in range(depth):
        a, q, k, v, p = _in_proj(xf, norm1_g[l][None], w_in_b, l, 2 * w_a, w_b, w_c)
        yb = _attention(q.reshape(b, s, w_b), k.reshape(b, s, w_b), v.reshape(b, s, w_b), mcat)
        sb_full = jnp.repeat(sgu_b[l].T, HEAD_DIM, axis=1)
        pw_bd = jax.scipy.linalg.block_diag(*[pool_w[l, g] for g in range(pool_w.shape[1])]).astype(BF16)
        x1 = _mixer_out(xf, a, yb.reshape(b * s, w_b), p, sgu_norm_g[l][None], sgu_w[l], sb_full, pw_bd,
                        pool_scale[l][None], mix_norm_g[l][None], w_o_b, l, gmat, s)
        xf = _ffn(x1, norm2_g[l][None], w_up_b, conv_w[l], conv_b[l][None], w_down_b, l, final_g[None], s,
                  l == depth - 1)
    return xf.reshape(b, s, d)
```
